```python
import math
import jax, jax.numpy as jnp
from jax import lax
import numpy as np

D_MODEL = 1024
BATCH = 8
SEQ = 4096
DEPTH = 1

GRID_W = 64
HEAD_DIM = 64
N_Q_HEADS = 8
N_KV_HEADS = 2
GQA_GROUP = N_Q_HEADS // N_KV_HEADS
ATTN_W = N_Q_HEADS * HEAD_DIM
KV_W = N_KV_HEADS * HEAD_DIM
Q_BLOCK = 128
ROPE_THETA = 10000.0
SSD_HEAD_DIM = 64
SSD_HEADS = 8
SSD_W = SSD_HEADS * SSD_HEAD_DIM
SSD_GROUPS = 2
HEADS_PER_GROUP = SSD_HEADS // SSD_GROUPS
D_STATE = 64
D_CONV = 7
CONV_CH = SSD_W + 2 * SSD_GROUPS * D_STATE
CHUNK = 128
SPLIT_SIZES = (ATTN_W, KV_W, KV_W, SSD_W, CONV_CH, 2 * SSD_HEADS, 2 * D_MODEL)
IN_PROJ = sum(SPLIT_SIZES)
N_EXPERT_GROUPS = 4
EXPERTS_PER_GROUP = 4
N_EXPERTS = N_EXPERT_GROUPS * EXPERTS_PER_GROUP
TOP_K_IN_GROUP = 2
D_EXPERT = 512
EPS = 1e-6

kernel_name = "hybrid_gqa_ssd_hiermoe_block"


def rmsnorm(x, w):
    xf = x.astype(jnp.float32)
    xf = xf * lax.rsqrt(jnp.mean(xf * xf, axis=-1, keepdims=True) + EPS)
    return xf.astype(x.dtype) * w


def rotate_half(u):
    u1, u2 = jnp.split(u, 2, axis=-1)
    return jnp.concatenate([-u2, u1], axis=-1)


def axial_rope_tables(seq_len, dtype):
    rows = seq_len // GRID_W
    row = jnp.repeat(jnp.arange(rows, dtype=jnp.int32), GRID_W)
    col = jnp.tile(jnp.arange(GRID_W, dtype=jnp.int32), rows)
    half = HEAD_DIM // 2
    inv_freq = ROPE_THETA ** (-jnp.arange(0, half, 2, dtype=jnp.float32) / half)
    ang_r = row.astype(jnp.float32)[:, None] * inv_freq[None, :]
    ang_c = col.astype(jnp.float32)[:, None] * inv_freq[None, :]
    cos = jnp.concatenate([jnp.cos(ang_r), jnp.cos(ang_r), jnp.cos(ang_c), jnp.cos(ang_c)], axis=-1)
    sin = jnp.concatenate([jnp.sin(ang_r), jnp.sin(ang_r), jnp.sin(ang_c), jnp.sin(ang_c)], axis=-1)
    return cos.astype(dtype), sin.astype(dtype)


def apply_axial_rope(x, cos, sin):
    xr, xc = jnp.split(x, 2, axis=-1)
    rot = jnp.concatenate([rotate_half(xr), rotate_half(xc)], axis=-1)
    return x * cos[None, :, None, :] + rot * sin[None, :, None, :]


def gqa_attention(q, k, v, q_norm_w, k_norm_w, cos, sin):
    b, s = q.shape[0], q.shape[1]
    q = apply_axial_rope(rmsnorm(q, q_norm_w), cos, sin)
    k = apply_axial_rope(rmsnorm(k, k_norm_w), cos, sin)
    scale = 1.0 / math.sqrt(HEAD_DIM)
    nb = s // Q_BLOCK
    qb = jnp.moveaxis(q.reshape(b, nb, Q_BLOCK, N_KV_HEADS, GQA_GROUP, HEAD_DIM), 1, 0)

    def one_block(qblk):
        scores = jnp.einsum('bqkgd,bskd->bkgqs', qblk, k).astype(jnp.float32) * scale
        p = jax.nn.softmax(scores, axis=-1).astype(v.dtype)
        return jnp.einsum('bkgqs,bskd->bqkgd', p, v)

    o = lax.map(one_block, qb)
    return jnp.moveaxis(o, 0, 1).reshape(b, s, ATTN_W)


def depthwise_centred_conv(u, w, bias):
    pad = (D_CONV - 1) // 2
    y = lax.conv_general_dilated(u, w[:, None, :], window_strides=(1,), padding=[(pad, pad)],
                                 dimension_numbers=('NWC', 'WIO', 'NWC'),
                                 feature_group_count=u.shape[-1])
    return y + bias


def ssd_chunked(x, dt, a, bm, cm):
    b, L, h, p = x.shape
    n = bm.shape[-1]
    c = L // CHUNK
    xc = x.reshape(b, c, CHUNK, h, p)
    dtc = dt.reshape(b, c, CHUNK, h)
    bc = bm.reshape(b, c, CHUNK, h, n)
    cc = cm.reshape(b, c, CHUNK, h, n)
    a_cum = jnp.cumsum(dtc * a, axis=2)
    seg = a_cum[:, :, :, None, :] - a_cum[:, :, None, :, :]
    tril = jnp.tril(jnp.ones((CHUNK, CHUNK), dtype=bool))[None, None, :, :, None]
    decay = jnp.exp(jnp.where(tril, seg, -jnp.inf))
    xdt = xc * dtc[..., None]
    cb = jnp.einsum('bcihn,bcjhn->bcijh', cc, bc)
    y_diag = jnp.einsum('bcijh,bcjhp->bcihp', cb * decay, xdt)
    decay_to_end = jnp.exp(a_cum[:, :, -1:, :] - a_cum)
    states = jnp.einsum('bcjhn,bcjhp->bchpn', bc * decay_to_end[..., None], xdt)
    chunk_decay = jnp.exp(a_cum[:, :, -1, :])

    def step(carry, inp):
        s_c, d_c = inp
        return carry * d_c[:, :, None, None] + s_c, carry

    h0 = jnp.zeros((b, h, p, n), dtype=x.dtype)
    _, states_in = lax.scan(step, h0, (jnp.moveaxis(states, 1, 0), jnp.moveaxis(chunk_decay, 1, 0)))
    states_in = jnp.moveaxis(states_in, 0, 1)
    y_off = jnp.einsum('bcihn,bchpn->bcihp', cc, states_in) * jnp.exp(a_cum)[..., None]
    return (y_diag + y_off).reshape(b, L, h, p)


def bidirectional_ssd(z, xbc, dt_raw, conv_w, conv_b, dt_bias, a_log, d_skip, norm_w):
    b, s = xbc.shape[0], xbc.shape[1]
    xbc = jax.nn.silu(depthwise_centred_conv(xbc, conv_w, conv_b))
    xs, bm, cm = jnp.split(xbc, [SSD_W, SSD_W + SSD_GROUPS * D_STATE], axis=-1)
    f32 = jnp.float32
    xs = xs.reshape(b, s, SSD_HEADS, SSD_HEAD_DIM).astype(f32)
    bm = jnp.repeat(bm.reshape(b, s, SSD_GROUPS, D_STATE), HEADS_PER_GROUP, axis=2).astype(f32)
    cm = jnp.repeat(cm.reshape(b, s, SSD_GROUPS, D_STATE), HEADS_PER_GROUP, axis=2).astype(f32)
    dt = jax.nn.softplus(dt_raw.reshape(b, s, 2, SSD_HEADS).astype(f32) + dt_bias.astype(f32))
    a = -jnp.exp(a_log.astype(f32))
    y_fwd = ssd_chunked(xs, dt[:, :, 0], a[0], bm, cm)
    flip = lambda u: jnp.flip(u, axis=1)
    y_bwd = flip(ssd_chunked(flip(xs), flip(dt[:, :, 1]), a[1], flip(bm), flip(cm)))
    y = y_fwd + y_bwd + d_skip.astype(f32)[None, None, :, None] * xs
    y = y.reshape(b, s, SSD_W).astype(z.dtype)
    return rmsnorm(y * jax.nn.silu(z), norm_w)


def hierarchical_moe(h, w_rg, b_rg, w_re, b_re, w1, w3, w2):
    b, s, d = h.shape
    t = h.reshape(b * s, d)
    p_group = jax.nn.softmax((t @ w_rg + b_rg).astype(jnp.float32), axis=-1)
    g_val, g_idx = lax.top_k(p_group, 1)
    fine = (t @ w_re + b_re).astype(jnp.float32).reshape(-1, N_EXPERT_GROUPS, EXPERTS_PER_GROUP)
    fine_sel = jnp.take_along_axis(fine, g_idx[:, :, None], axis=1)[:, 0]
    e_val, e_idx = lax.top_k(jax.nn.softmax(fine_sel, axis=-1), TOP_K_IN_GROUP)
    e_val = e_val / jnp.sum(e_val, axis=-1, keepdims=True)
    within = jnp.sum(jax.nn.one_hot(e_idx, EXPERTS_PER_GROUP, dtype=jnp.float32) * e_val[..., None], axis=1)
    combine = (jax.nn.one_hot(g_idx[:, 0], N_EXPERT_GROUPS, dtype=jnp.float32)[:, :, None]
               * within[:, None, :] * g_val[:, :, None]).reshape(-1, N_EXPERTS).astype(t.dtype)
    out = jnp.zeros_like(t)
    for e in range(N_EXPERTS):
        y_e = (jax.nn.silu(t @ w1[e]) * (t @ w3[e])) @ w2[e]
        out = out + combine[:, e:e + 1] * y_e
    return out.reshape(b, s, d)


def setup_inputs(seed: int = 0) -> dict:
    key = jax.random.key(seed)
    ks = jax.random.split(key, 24)
    f32 = jnp.float32
    L = DEPTH

    def nrm(k, shape, fan_in):
        return jax.random.normal(k, shape, f32) * (fan_in ** -0.5)

    def gain(k, shape):
        return 1.0 + 0.02 * jax.random.normal(k, shape, f32)

    dt0 = jnp.exp(jax.random.uniform(ks[9], (L, 2, SSD_HEADS), f32, math.log(1e-3), math.log(1e-1)))
    dt_bias = dt0 + jnp.log(-jnp.expm1(-dt0))
    return {
        "x": jax.random.normal(ks[0], (BATCH, SEQ, D_MODEL), f32),
        "norm_mix_w": gain(ks[1], (L, D_MODEL)),
        "w_in": nrm(ks[2], (L, D_MODEL, IN_PROJ), D_MODEL),
        "b_gate": 0.02 * jax.random.normal(ks[3], (L, 2 * D_MODEL), f32),
        "q_norm_w": gain(ks[4], (L, HEAD_DIM)),
        "k_norm_w": gain(ks[5], (L, HEAD_DIM)),
        "w_attn_o": nrm(ks[6], (L, ATTN_W, D_MODEL), ATTN_W),
        "conv_w": nrm(ks[7], (L, D_CONV, CONV_CH), D_CONV),
        "conv_b": 0.02 * jax.random.normal(ks[8], (L, CONV_CH), f32),
        "dt_bias": dt_bias,
        "a_log": jnp.log(jax.random.uniform(ks[10], (L, 2, SSD_HEADS), f32, 1.0, 16.0)),
        "d_skip": 1.0 + 0.1 * jax.random.normal(ks[11], (L, SSD_HEADS), f32),
        "ssd_norm_w": gain(ks[12], (L, SSD_W)),
        "w_ssd_o": nrm(ks[13], (L, SSD_W, D_MODEL), SSD_W),
        "w_out": nrm(ks[14], (L, D_MODEL, D_MODEL), D_MODEL),
        "norm_ffn_w": gain(ks[15], (L, D_MODEL)),
        "w_router_group": nrm(ks[16], (L, D_MODEL, N_EXPERT_GROUPS), D_MODEL),
        "b_router_group": 0.01 * jax.random.normal(ks[17], (L, N_EXPERT_GROUPS), f32),
        "w_router_expert": nrm(ks[18], (L, D_MODEL, N_EXPERTS), D_MODEL),
        "b_router_expert": 0.01 * jax.random.normal(ks[19], (L, N_EXPERTS), f32),
        "w1": nrm(ks[20], (L, N_EXPERTS, D_MODEL, D_EXPERT), D_MODEL),
        "w3": nrm(ks[21], (L, N_EXPERTS, D_MODEL, D_EXPERT), D_MODEL),
        "w2": nrm(ks[22], (L, N_EXPERTS, D_EXPERT, D_MODEL), D_EXPERT),
    }


def reference(x, norm_mix_w, w_in, b_gate, q_norm_w, k_norm_w, w_attn_o, conv_w, conv_b,
              dt_bias, a_log, d_skip, ssd_norm_w, w_ssd_o, w_out, norm_ffn_w,
              w_router_group, b_router_group, w_router_expert, b_router_expert, w1, w3, w2):
    b, s, _ = x.shape
    cos, sin = axial_rope_tables(s, x.dtype)
    split_at = [int(v) for v in np.cumsum(SPLIT_SIZES)[:-1]]
    for l in range(DEPTH):
        h = rmsnorm(x, norm_mix_w[l])
        proj = h @ w_in[l]
        q, k, v, z, xbc, dt_raw, gates = jnp.split(proj, split_at, axis=-1)
        q = q.reshape(b, s, N_Q_HEADS, HEAD_DIM)
        k = k.reshape(b, s, N_KV_HEADS, HEAD_DIM)
        v = v.reshape(b, s, N_KV_HEADS, HEAD_DIM)
        attn = gqa_attention(q, k, v, q_norm_w[l], k_norm_w[l], cos, sin)
        ssd = bidirectional_ssd(z, xbc, dt_raw, conv_w[l], conv_b[l], dt_bias[l],
                                a_log[l], d_skip[l], ssd_norm_w[l])
        g_attn, g_ssd = jnp.split(jax.nn.sigmoid(gates + b_gate[l]), 2, axis=-1)
        merged = g_attn * (attn @ w_attn_o[l]) + g_ssd * (ssd @ w_ssd_o[l])
        x = x + merged @ w_out[l]
        h2 = rmsnorm(x, norm_ffn_w[l])
        x = x + hierarchical_moe(h2, w_router_group[l], b_router_group[l], w_router_expert[l],
                                 b_router_expert[l], w1[l], w3[l], w2[l])
    return x
```

```python
import functools
import math

import jax
import jax.numpy as jnp
from jax import lax
from jax.experimental import pallas as pl
from jax.experimental.pallas import tpu as pltpu

GRID_W = 64
HEAD_DIM = 64
N_Q_HEADS = 8
N_KV_HEADS = 2
GQA_GROUP = N_Q_HEADS // N_KV_HEADS
ATTN_W = N_Q_HEADS * HEAD_DIM
KV_W = N_KV_HEADS * HEAD_DIM
ROPE_THETA = 10000.0
SSD_HEAD_DIM = 64
SSD_HEADS = 8
SSD_W = SSD_HEADS * SSD_HEAD_DIM
SSD_GROUPS = 2
HEADS_PER_GROUP = SSD_HEADS // SSD_GROUPS
D_STATE = 64
D_CONV = 7
CONV_PAD = (D_CONV - 1) // 2
CONV_CH = SSD_W + 2 * SSD_GROUPS * D_STATE
CHUNK = 128
N_EXPERT_GROUPS = 4
EXPERTS_PER_GROUP = 4
N_EXPERTS = N_EXPERT_GROUPS * EXPERTS_PER_GROUP
EPS = 1e-6

LANES = 128
SUBLANES = 8
VMEM_LIMIT = 56 * 1024 * 1024

TM_PROJ = 256
TQ = 256
TM_MOE = 512

F32 = jnp.float32
BF16 = jnp.bfloat16
HIGHEST = lax.Precision.HIGHEST


def _dot(a, b):
    return jnp.dot(a, b, preferred_element_type=F32)


def _dot_nt(a, b):
    return lax.dot_general(a, b, (((1,), (1,)), ((), ())), preferred_element_type=F32)


def _dot_tn(a, b):
    return lax.dot_general(a, b, (((0,), (0,)), ((), ())), preferred_element_type=F32)


def _sigmoid(x):
    return 1.0 / (1.0 + jnp.exp(-x))


def _softplus(x):
    return jnp.maximum(x, 0.0) + jnp.log1p(jnp.exp(-jnp.abs(x)))


def _params(n_axes):
    return pltpu.CompilerParams(dimension_semantics=("arbitrary",) * n_axes,
                                vmem_limit_bytes=VMEM_LIMIT)


def _full(shape):
    return pl.BlockSpec(shape, lambda *_: (0,) * len(shape))


def _in_proj_kernel(x_ref, nw_ref, wq_ref, wk_ref, wv_ref, wz_ref, wxbc_ref, wdt_ref, wg_ref, bg_ref,
                    q_ref, k_ref, v_ref, z_ref, xbc_ref, dt_ref, g_ref):
    x = x_ref[...]
    ms = jnp.mean(x * x, axis=-1, keepdims=True)
    h = (x * lax.rsqrt(ms + EPS) * nw_ref[...]).astype(BF16)
    q_ref[...] = _dot(h, wq_ref[...])
    k = _dot(h, wk_ref[...])
    v = _dot(h, wv_ref[...]).astype(BF16)
    for j in range(N_KV_HEADS):
        k_ref[j] = k[:, j * HEAD_DIM:(j + 1) * HEAD_DIM]
        v_ref[j] = v[:, j * HEAD_DIM:(j + 1) * HEAD_DIM]
    z_ref[...] = _dot(h, wz_ref[...])
    xbc_ref[...] = _dot(h, wxbc_ref[...])
    dt_ref[...] = _dot(h, wdt_ref[...])
    g_ref[...] = _sigmoid(_dot(h, wg_ref[...]) + bg_ref[...]).astype(BF16)


def _in_proj(x2d, nw, wq, wk, wv, wz, wxbc, wdt, wg, bg):
    t, d = x2d.shape
    tm = TM_PROJ
    row = lambda n: pl.BlockSpec((tm, n), lambda i: (i, 0))
    return pl.pallas_call(
        _in_proj_kernel,
        grid=(t // tm,),
        in_specs=[row(d), _full(nw.shape), _full(wq.shape), _full(wk.shape), _full(wv.shape),
                  _full(wz.shape), _full(wxbc.shape), _full(wdt.shape), _full(wg.shape), _full(bg.shape)],
        out_specs=[row(ATTN_W),
                   pl.BlockSpec((N_KV_HEADS, tm, HEAD_DIM), lambda i: (0, i, 0)),
                   pl.BlockSpec((N_KV_HEADS, tm, HEAD_DIM), lambda i: (0, i, 0)),
                   row(SSD_W), row(CONV_CH), row(LANES), row(wg.shape[1])],
        out_shape=[jax.ShapeDtypeStruct((t, ATTN_W), F32),
                   jax.ShapeDtypeStruct((N_KV_HEADS, t, HEAD_DIM), F32),
                   jax.ShapeDtypeStruct((N_KV_HEADS, t, HEAD_DIM), BF16),
                   jax.ShapeDtypeStruct((t, SSD_W), F32),
                   jax.ShapeDtypeStruct((t, CONV_CH), F32),
                   jax.ShapeDtypeStruct((t, LANES), F32),
                   jax.ShapeDtypeStruct((t, wg.shape[1]), BF16)],
        compiler_params=_params(1),
        name="in_proj",
    )(x2d, nw, wq, wk, wv, wz, wxbc, wdt, wg, bg)


def _rope_partner(u):
    q = HEAD_DIM // 4
    return jnp.concatenate([u[:, q:2 * q], u[:, 0:q], u[:, 3 * q:4 * q], u[:, 2 * q:3 * q]], axis=-1)


def _norm_rope(u, w, cos, sin_signed):
    ms = jnp.mean(u * u, axis=-1, keepdims=True)
    un = u * lax.rsqrt(ms + EPS) * w
    return un * cos + _rope_partner(un) * sin_signed


def _attn_kernel(q_ref, k_ref, v_ref, cosq_ref, sinq_ref, cosk_ref, sink_ref, qw_ref, kw_ref,
                 o_ref, ks_ref):
    @pl.when(pl.program_id(2) == 0)
    def _():
        ks_ref[...] = _norm_rope(k_ref[0], kw_ref[...], cosk_ref[...], sink_ref[...]).astype(BF16)

    q = q_ref[0]
    cos = cosq_ref[...]
    sin = sinq_ref[...]
    scale = 1.0 / math.sqrt(HEAD_DIM)
    outs = []
    for g in range(GQA_GROUP):
        qg = _norm_rope(q[:, g * HEAD_DIM:(g + 1) * HEAD_DIM], qw_ref[...], cos, sin) * scale
        s = _dot_nt(qg.astype(BF16), ks_ref[...])
        m = jnp.max(s, axis=-1, keepdims=True)
        p = jnp.exp(s - m)
        l = jnp.sum(p, axis=-1, keepdims=True)
        outs.append(_dot(p.astype(BF16), v_ref[0]) / l)
    o_ref[0] = jnp.concatenate(outs, axis=-1).astype(BF16)


def _attention(q, k, v, cos, sin_signed, qw, kw, b, s):
    tq = TQ
    gw = GQA_GROUP * HEAD_DIM
    return pl.pallas_call(
        _attn_kernel,
        grid=(b, N_KV_HEADS, s // tq),
        in_specs=[pl.BlockSpec((1, tq, gw), lambda bi, kv, qi: (bi, qi, kv)),
                  pl.BlockSpec((1, s, HEAD_DIM), lambda bi, kv, qi: (kv, bi, 0)),
                  pl.BlockSpec((1, s, HEAD_DIM), lambda bi, kv, qi: (kv, bi, 0)),
                  pl.BlockSpec((tq, HEAD_DIM), lambda bi, kv, qi: (qi, 0)),
                  pl.BlockSpec((tq, HEAD_DIM), lambda bi, kv, qi: (qi, 0)),
                  _full((s, HEAD_DIM)), _full((s, HEAD_DIM)),
                  _full((1, HEAD_DIM)), _full((1, HEAD_DIM))],
        out_specs=pl.BlockSpec((1, tq, gw), lambda bi, kv, qi: (bi, qi, kv)),
        out_shape=jax.ShapeDtypeStruct((b, s, ATTN_W), BF16),
        scratch_shapes=[pltpu.VMEM((s, HEAD_DIM), BF16)],
        compiler_params=_params(3),
        name="attention",
    )(q, k, v, cos, sin_signed, cos, sin_signed, qw, kw)


def _ssd_kernel(nc, cur_ref, prev_ref, next_ref, cw_ref, cb_ref, dtc_ref, dtr_ref, bc_ref, br_ref,
                ac_ref, ar_ref, m_ref, mt_ref, dskip_ref, y_ref, state_ref):
    d = pl.program_id(0)
    c = pl.program_id(2)
    cp = c + d * (nc - 1 - 2 * c)

    @pl.when(c == 0)
    def _():
        state_ref[...] = jnp.zeros_like(state_ref)

    prev = jnp.where(cp > 0, prev_ref[0], 0.0)
    nxt = jnp.where(cp < nc - 1, next_ref[0], 0.0)
    xin = jnp.concatenate([prev, cur_ref[0], nxt], axis=0)
    acc = jnp.zeros((CHUNK, CONV_CH), F32) + cb_ref[...]
    for t in range(D_CONV):
        off = SUBLANES - CONV_PAD + t
        acc = acc + xin[off:off + CHUNK, :] * cw_ref[t:t + 1, :]
    xc = acc * _sigmoid(acc)

    xs = xc[:, :SSD_W]
    dt_c = _softplus(dtc_ref[0, 0] + bc_ref[0])
    dta_c = dt_c * (-jnp.exp(ac_ref[0]))
    dta_r = _softplus(dtr_ref[0, 0] + br_ref[0]) * (-jnp.exp(ar_ref[0]))
    mask = m_ref[0]
    u_c = jnp.dot(mask, dta_c, precision=HIGHEST, preferred_element_type=F32)
    u_r = jnp.dot(dta_r, mt_ref[0], precision=HIGHEST, preferred_element_type=F32)
    tot = jnp.sum(dta_c, axis=0, keepdims=True)

    ys = []
    for g in range(SSD_GROUPS):
        bg = xc[:, SSD_W + g * D_STATE:SSD_W + (g + 1) * D_STATE]
        cg = xc[:, SSD_W + SSD_GROUPS * D_STATE + g * D_STATE:SSD_W + SSD_GROUPS * D_STATE + (g + 1) * D_STATE]
        cg16 = cg.astype(BF16)
        cb = _dot_nt(cg16, bg.astype(BF16))
        for hh in range(HEADS_PER_GROUP):
            h = g * HEADS_PER_GROUP + hh
            uc = u_c[:, h:h + 1]
            seg = uc - u_r[h:h + 1, :]
            decay = jnp.exp(jnp.where(mask > 0.0, seg, -jnp.inf))
            xdt = (xs[:, h * SSD_HEAD_DIM:(h + 1) * SSD_HEAD_DIM] * dt_c[:, h:h + 1]).astype(BF16)
            st = state_ref[h]
            y = _dot((cb * decay).astype(BF16), xdt) + _dot(cg16, st.astype(BF16)) * jnp.exp(uc)
            ys.append(y)
            w_end = jnp.exp(tot[:, h:h + 1] - uc)
            state_ref[h] = st * jnp.exp(tot[:, h:h + 1]) + _dot_tn((bg * w_end).astype(BF16), xdt)
    skip = jnp.where(d == 0, 1.0, 0.0) * dskip_ref[...]
    y_ref[0, 0] = jnp.concatenate(ys, axis=-1) + skip * xs


def _ssd(xbc, dt_col, dt_row, conv_w, conv_b, bias_c, bias_r, alog_c, alog_r, masks, dskip, b, s):
    nc = s // CHUNK
    hb = CHUNK // SUBLANES
    nhb = s // SUBLANES
    pos = lambda d, c: c + d * (nc - 1 - 2 * c)
    return pl.pallas_call(
        functools.partial(_ssd_kernel, nc),
        grid=(2, b, nc),
        in_specs=[
            pl.BlockSpec((1, CHUNK, CONV_CH), lambda d, bi, c: (bi, pos(d, c), 0)),
            pl.BlockSpec((1, SUBLANES, CONV_CH), lambda d, bi, c: (bi, jnp.maximum(pos(d, c) * hb - 1, 0), 0)),
            pl.BlockSpec((1, SUBLANES, CONV_CH),
                         lambda d, bi, c: (bi, jnp.minimum((pos(d, c) + 1) * hb, nhb - 1), 0)),
            _full(conv_w.shape), _full(conv_b.shape),
            pl.BlockSpec((1, 1, CHUNK, SSD_HEADS), lambda d, bi, c: (d, bi, pos(d, c), 0)),
            pl.BlockSpec((1, 1, SSD_HEADS, CHUNK), lambda d, bi, c: (d, bi, 0, pos(d, c))),
            pl.BlockSpec((1, 1, SSD_HEADS), lambda d, bi, c: (d, 0, 0)),
            pl.BlockSpec((1, SSD_HEADS, 1), lambda d, bi, c: (d, 0, 0)),
            pl.BlockSpec((1, 1, SSD_HEADS), lambda d, bi, c: (d, 0, 0)),
            pl.BlockSpec((1, SSD_HEADS, 1), lambda d, bi, c: (d, 0, 0)),
            pl.BlockSpec((1, CHUNK, CHUNK), lambda d, bi, c: (d, 0, 0)),
            pl.BlockSpec((1, CHUNK, CHUNK), lambda d, bi, c: (1 - d, 0, 0)),
            _full(dskip.shape),
        ],
        out_specs=pl.BlockSpec((1, 1, CHUNK, SSD_W), lambda d, bi, c: (d, bi, pos(d, c), 0)),
        out_shape=jax.ShapeDtypeStruct((2, b, s, SSD_W), F32),
        scratch_shapes=[pltpu.VMEM((SSD_HEADS, D_STATE, SSD_HEAD_DIM), F32)],
        compiler_params=_params(3),
        name="ssd",
    )(xbc, xbc, xbc, conv_w, conv_b, dt_col, dt_row, bias_c, bias_r, alog_c, alog_r, masks, masks, dskip)


def _out_proj_kernel(x_ref, yf_ref, yb_ref, z_ref, attn_ref, g_ref, snw_ref, wso_ref, wao_ref, wout_ref,
                     fnw_ref, wr_ref, br_ref, x2_ref, h2_ref, comb_ref):
    z = z_ref[...]
    y = (yf_ref[0] + yb_ref[0]) * (z * _sigmoid(z))
    ms = jnp.mean(y * y, axis=-1, keepdims=True)
    ssd = (y * lax.rsqrt(ms + EPS) * snw_ref[...]).astype(BF16)
    g = g_ref[...]
    dm = x_ref.shape[1]
    merged = g[:, :dm] * _dot(attn_ref[...], wao_ref[...]) + g[:, dm:] * _dot(ssd, wso_ref[...])
    x2 = x_ref[...] + _dot(merged.astype(BF16), wout_ref[...])
    x2_ref[...] = x2

    ms2 = jnp.mean(x2 * x2, axis=-1, keepdims=True)
    h2 = x2 * lax.rsqrt(ms2 + EPS) * fnw_ref[...]
    h2_ref[...] = h2.astype(BF16)

    lg = jnp.dot(h2, wr_ref[...], precision=HIGHEST, preferred_element_type=F32) + br_ref[...]
    lane = lax.broadcasted_iota(jnp.int32, lg.shape, 1)
    neg = -jnp.inf
    big = jnp.int32(LANES)
    is_g = (lane >= N_EXPERTS) & (lane < N_EXPERTS + N_EXPERT_GROUPS)
    gl = jnp.where(is_g, lg, neg)
    ge = jnp.exp(gl - jnp.max(gl, axis=-1, keepdims=True))
    pg = ge / jnp.sum(ge, axis=-1, keepdims=True)
    g_val = jnp.max(pg, axis=-1, keepdims=True)
    g_idx = jnp.min(jnp.where(is_g & (pg == g_val), lane, big), axis=-1, keepdims=True) - N_EXPERTS
    lo = g_idx * EXPERTS_PER_GROUP
    sel = (lane >= lo) & (lane < lo + EXPERTS_PER_GROUP)
    fl = jnp.where(sel, lg, neg)
    fe = jnp.exp(fl - jnp.max(fl, axis=-1, keepdims=True))
    pf = fe / jnp.sum(fe, axis=-1, keepdims=True)
    v1 = jnp.max(pf, axis=-1, keepdims=True)
    i1 = jnp.min(jnp.where(sel & (pf == v1), lane, big), axis=-1, keepdims=True)
    pf2 = jnp.where(sel & (lane != i1), pf, -1.0)
    v2 = jnp.max(pf2, axis=-1, keepdims=True)
    i2 = jnp.min(jnp.where(pf2 == v2, lane, big), axis=-1, keepdims=True)
    den = v1 + v2
    comb_ref[...] = g_val * jnp.where(lane == i1, v1 / den, jnp.where(lane == i2, v2 / den, 0.0))


def _out_proj(x2d, y, z, attn, g, snw, wso, wao, wout, fnw, wr, br):
    t, d = x2d.shape
    tm = TM_PROJ
    row = lambda n: pl.BlockSpec((tm, n), lambda i: (i, 0))
    return pl.pallas_call(
        _out_proj_kernel,
        grid=(t // tm,),
        in_specs=[row(d),
                  pl.BlockSpec((1, tm, SSD_W), lambda i: (0, i, 0)),
                  pl.BlockSpec((1, tm, SSD_W), lambda i: (1, i, 0)),
                  row(SSD_W), row(ATTN_W), row(2 * d),
                  _full(snw.shape), _full(wso.shape), _full(wao.shape), _full(wout.shape),
                  _full(fnw.shape), _full(wr.shape), _full(br.shape)],
        out_specs=[row(d), row(d), row(LANES)],
        out_shape=[jax.ShapeDtypeStruct((t, d), F32),
                   jax.ShapeDtypeStruct((t, d), BF16),
                   jax.ShapeDtypeStruct((t, LANES), F32)],
        compiler_params=_params(1),
        name="out_proj",
    )(x2d, y, y, z, attn, g, snw, wso, wao, wout, fnw, wr, br)


def _moe_kernel(x2_ref, h2_ref, comb_ref, w1_ref, w3_ref, w2_ref, o_ref):
    e = pl.program_id(1)

    @pl.when(e == 0)
    def _():
        o_ref[...] = x2_ref[...]

    h = h2_ref[...]
    a = _dot(h, w1_ref[0])
    act = (a * _sigmoid(a)) * _dot(h, w3_ref[0])
    comb = comb_ref[...]
    lane = lax.broadcasted_iota(jnp.int32, comb.shape, 1)
    ce = jnp.sum(jnp.where(lane == e, comb, 0.0), axis=-1, keepdims=True)
    o_ref[...] += ce * _dot(act.astype(BF16), w2_ref[0])


def _moe(x2, h2, comb, w1, w3, w2):
    t, d = x2.shape
    tm = TM_MOE
    de = w1.shape[2]
    row = lambda n: pl.BlockSpec((tm, n), lambda i, e: (i, 0))
    return pl.pallas_call(
        _moe_kernel,
        grid=(t // tm, N_EXPERTS),
        in_specs=[row(d), row(d), row(LANES),
                  pl.BlockSpec((1, d, de), lambda i, e: (e, 0, 0)),
                  pl.BlockSpec((1, d, de), lambda i, e: (e, 0, 0)),
                  pl.BlockSpec((1, de, d), lambda i, e: (e, 0, 0))],
        out_specs=row(d),
        out_shape=jax.ShapeDtypeStruct((t, d), F32),
        compiler_params=_params(2),
        name="moe",
    )(x2, h2, comb, w1, w3, w2)


def _rope_tables(s):
    rows = s // GRID_W
    row = jnp.repeat(jnp.arange(rows, dtype=jnp.int32), GRID_W)
    col = jnp.tile(jnp.arange(GRID_W, dtype=jnp.int32), rows)
    half = HEAD_DIM // 2
    inv_freq = ROPE_THETA ** (-jnp.arange(0, half, 2, dtype=F32) / half)
    ang_r = row.astype(F32)[:, None] * inv_freq[None, :]
    ang_c = col.astype(F32)[:, None] * inv_freq[None, :]
    cos = jnp.concatenate([jnp.cos(ang_r), jnp.cos(ang_r), jnp.cos(ang_c), jnp.cos(ang_c)], axis=-1)
    sin_signed = jnp.concatenate([-jnp.sin(ang_r), jnp.sin(ang_r), -jnp.sin(ang_c), jnp.sin(ang_c)], axis=-1)
    return cos, sin_signed


def kernel(x, norm_mix_w, w_in, b_gate, q_norm_w, k_norm_w, w_attn_o, conv_w, conv_b, dt_bias, a_log, d_skip,
           ssd_norm_w, w_ssd_o, w_out, norm_ffn_w, w_router_group, b_router_group, w_router_expert,
           b_router_expert, w1, w3, w2):
    b, s, d = x.shape
    t = b * s
    depth = norm_mix_w.shape[0]
    cos, sin_signed = _rope_tables(s)
    tri = jnp.tril(jnp.ones((CHUNK, CHUNK), F32))
    masks = jnp.stack([tri, tri.T])
    x2d = x.reshape(t, d)
    sizes = (ATTN_W, KV_W, KV_W, SSD_W, CONV_CH, 2 * SSD_HEADS, 2 * d)
    offs = [0]
    for n in sizes:
        offs.append(offs[-1] + n)

    for l in range(depth):
        wi = w_in[l].astype(BF16)
        wq, wk, wv, wz, wxbc, wdt, wg = (wi[:, offs[j]:offs[j + 1]] for j in range(7))
        wdt = jnp.pad(wdt, ((0, 0), (0, LANES - 2 * SSD_HEADS)))
        q, k, v, z, xbc, dt_raw, g = _in_proj(x2d, norm_mix_w[l][None], wq, wk, wv, wz, wxbc, wdt, wg,
                                              b_gate[l][None])

        attn = _attention(q.reshape(b, s, ATTN_W), k, v, cos, sin_signed, q_norm_w[l][None], k_norm_w[l][None],
                          b, s)

        dt4 = dt_raw[:, :2 * SSD_HEADS].reshape(b, s, 2, SSD_HEADS)
        dt_col = dt4.transpose(2, 0, 1, 3)
        dt_row = dt4.transpose(2, 0, 3, 1)
        cw = jnp.pad(conv_w[l], ((0, SUBLANES - D_CONV), (0, 0)))
        y = _ssd(xbc.reshape(b, s, CONV_CH), dt_col, dt_row, cw, conv_b[l][None],
                 dt_bias[l][:, None, :], dt_bias[l][:, :, None], a_log[l][:, None, :], a_log[l][:, :, None],
                 masks, jnp.repeat(d_skip[l], SSD_HEAD_DIM)[None], b, s)

        wr = jnp.concatenate([w_router_expert[l], w_router_group[l]], axis=1)
        wr = jnp.pad(wr, ((0, 0), (0, LANES - wr.shape[1])))
        br = jnp.concatenate([b_router_expert[l], b_router_group[l]])
        br = jnp.pad(br, (0, LANES - br.shape[0]))[None]
        x2, h2, comb = _out_proj(x2d, y.reshape(2, t, SSD_W), z, attn.reshape(t, ATTN_W), g,
                                 ssd_norm_w[l][None], w_ssd_o[l].astype(BF16), w_attn_o[l].astype(BF16),
                                 w_out[l].astype(BF16), norm_ffn_w[l][None], wr, br)

        x2d = _moe(x2, h2, comb, w1[l].astype(BF16), w3[l].astype(BF16), w2[l].astype(BF16))
    return x2d.reshape(b, s, d)
```

```python
import functools
import math

import jax
import jax.numpy as jnp
from jax import lax
from jax.experimental import pallas as pl
from jax.experimental.pallas import tpu as pltpu

GRID_W = 64
HEAD_DIM = 64
N_Q_HEADS = 8
N_KV_HEADS = 2
GQA_GROUP = N_Q_HEADS // N_KV_HEADS
ATTN_W = N_Q_HEADS * HEAD_DIM
KV_W = N_KV_HEADS * HEAD_DIM
ROPE_THETA = 10000.0
SSD_HEAD_DIM = 64
SSD_HEADS = 8
SSD_W = SSD_HEADS * SSD_HEAD_DIM
SSD_GROUPS = 2
HEADS_PER_GROUP = SSD_HEADS // SSD_GROUPS
GROUP_W = HEADS_PER_GROUP * SSD_HEAD_DIM
D_STATE = 64
D_CONV = 7
CONV_PAD = (D_CONV - 1) // 2
CONV_CH = SSD_W + 2 * SSD_GROUPS * D_STATE
CHUNK = 128
N_EXPERT_GROUPS = 4
EXPERTS_PER_GROUP = 4
N_EXPERTS = N_EXPERT_GROUPS * EXPERTS_PER_GROUP
EPS = 1e-6

LANES = 128
SUBLANES = 8
VMEM_LIMIT = 56 * 1024 * 1024

TM_PROJ = 256
TQ = 256
TM_MOE = 512

F32 = jnp.float32
BF16 = jnp.bfloat16
HIGHEST = lax.Precision.HIGHEST


def _dot(a, b):
    return jnp.dot(a, b, preferred_element_type=F32)


def _dot_nt(a, b):
    return lax.dot_general(a, b, (((1,), (1,)), ((), ())), preferred_element_type=F32)


def _dot_tn(a, b):
    return lax.dot_general(a, b, (((0,), (0,)), ((), ())), preferred_element_type=F32)


def _sigmoid(x):
    return 1.0 / (1.0 + jnp.exp(-x))


def _softplus(x):
    return jnp.maximum(x, 0.0) + jnp.log1p(jnp.exp(-jnp.abs(x)))


def _params(n_axes):
    return pltpu.CompilerParams(dimension_semantics=("arbitrary",) * n_axes,
                                vmem_limit_bytes=VMEM_LIMIT)


def _full(shape):
    return pl.BlockSpec(shape, lambda *_: (0,) * len(shape))


def _in_proj_kernel(x_ref, nw_ref, wq_ref, wk_ref, wv_ref, wz_ref, wxbc_ref, wdt_ref, wg_ref, bg_ref,
                    q_ref, k_ref, v_ref, z_ref, xbc_ref, dt_ref, g_ref):
    x = x_ref[...]
    ms = jnp.mean(x * x, axis=-1, keepdims=True)
    h = (x * lax.rsqrt(ms + EPS) * nw_ref[...]).astype(BF16)
    q_ref[...] = _dot(h, wq_ref[...])
    k = _dot(h, wk_ref[...])
    v = _dot(h, wv_ref[...]).astype(BF16)
    for j in range(N_KV_HEADS):
        k_ref[j] = k[:, j * HEAD_DIM:(j + 1) * HEAD_DIM]
        v_ref[j] = v[:, j * HEAD_DIM:(j + 1) * HEAD_DIM]
    z_ref[...] = _dot(h, wz_ref[...])
    xbc_ref[...] = _dot(h, wxbc_ref[...])
    dt_ref[...] = _dot(h, wdt_ref[...])
    g_ref[...] = _sigmoid(_dot(h, wg_ref[...]) + bg_ref[...]).astype(BF16)


def _in_proj(x2d, nw, wq, wk, wv, wz, wxbc, wdt, wg, bg):
    t, d = x2d.shape
    tm = TM_PROJ
    row = lambda n: pl.BlockSpec((tm, n), lambda i: (i, 0))
    return pl.pallas_call(
        _in_proj_kernel,
        grid=(t // tm,),
        in_specs=[row(d), _full(nw.shape), _full(wq.shape), _full(wk.shape), _full(wv.shape),
                  _full(wz.shape), _full(wxbc.shape), _full(wdt.shape), _full(wg.shape), _full(bg.shape)],
        out_specs=[row(ATTN_W),
                   pl.BlockSpec((N_KV_HEADS, tm, HEAD_DIM), lambda i: (0, i, 0)),
                   pl.BlockSpec((N_KV_HEADS, tm, HEAD_DIM), lambda i: (0, i, 0)),
                   row(SSD_W), row(CONV_CH), row(LANES), row(wg.shape[1])],
        out_shape=[jax.ShapeDtypeStruct((t, ATTN_W), F32),
                   jax.ShapeDtypeStruct((N_KV_HEADS, t, HEAD_DIM), F32),
                   jax.ShapeDtypeStruct((N_KV_HEADS, t, HEAD_DIM), BF16),
                   jax.ShapeDtypeStruct((t, SSD_W), F32),
                   jax.ShapeDtypeStruct((t, CONV_CH), F32),
                   jax.ShapeDtypeStruct((t, LANES), F32),
                   jax.ShapeDtypeStruct((t, wg.shape[1]), BF16)],
        compiler_params=_params(1),
        name="in_proj",
    )(x2d, nw, wq, wk, wv, wz, wxbc, wdt, wg, bg)


def _rope_partner(u):
    q = HEAD_DIM // 4
    return jnp.concatenate([u[:, q:2 * q], u[:, 0:q], u[:, 3 * q:4 * q], u[:, 2 * q:3 * q]], axis=-1)


def _norm_rope(u, w, cos, sin_signed):
    ms = jnp.mean(u * u, axis=-1, keepdims=True)
    un = u * lax.rsqrt(ms + EPS) * w
    return un * cos + _rope_partner(un) * sin_signed


def _attn_kernel(q_ref, k_ref, v_ref, cosq_ref, sinq_ref, cosk_ref, sink_ref, qw_ref, kw_ref,
                 o_ref, ks_ref):
    @pl.when(pl.program_id(2) == 0)
    def _():
        ks_ref[...] = _norm_rope(k_ref[0], kw_ref[...], cosk_ref[...], sink_ref[...]).astype(BF16)

    q = q_ref[0]
    cos = cosq_ref[...]
    sin = sinq_ref[...]
    scale = 1.0 / math.sqrt(HEAD_DIM)
    outs = []
    for g in range(GQA_GROUP):
        qg = _norm_rope(q[:, g * HEAD_DIM:(g + 1) * HEAD_DIM], qw_ref[...], cos, sin) * scale
        s = _dot_nt(qg.astype(BF16), ks_ref[...])
        m = jnp.max(s, axis=-1, keepdims=True)
        p = jnp.exp(s - m)
        l = jnp.sum(p, axis=-1, keepdims=True)
        outs.append(_dot(p.astype(BF16), v_ref[0]) / l)
    o_ref[0] = jnp.concatenate(outs, axis=-1).astype(BF16)


def _attention(q, k, v, cos, sin_signed, qw, kw, b, s):
    tq = TQ
    gw = GQA_GROUP * HEAD_DIM
    return pl.pallas_call(
        _attn_kernel,
        grid=(b, N_KV_HEADS, s // tq),
        in_specs=[pl.BlockSpec((1, tq, gw), lambda bi, kv, qi: (bi, qi, kv)),
                  pl.BlockSpec((1, s, HEAD_DIM), lambda bi, kv, qi: (kv, bi, 0)),
                  pl.BlockSpec((1, s, HEAD_DIM), lambda bi, kv, qi: (kv, bi, 0)),
                  pl.BlockSpec((tq, HEAD_DIM), lambda bi, kv, qi: (qi, 0)),
                  pl.BlockSpec((tq, HEAD_DIM), lambda bi, kv, qi: (qi, 0)),
                  _full((s, HEAD_DIM)), _full((s, HEAD_DIM)),
                  _full((1, HEAD_DIM)), _full((1, HEAD_DIM))],
        out_specs=pl.BlockSpec((1, tq, gw), lambda bi, kv, qi: (bi, qi, kv)),
        out_shape=jax.ShapeDtypeStruct((b, s, ATTN_W), BF16),
        scratch_shapes=[pltpu.VMEM((s, HEAD_DIM), BF16)],
        compiler_params=_params(3),
        name="attention",
    )(q, k, v, cos, sin_signed, cos, sin_signed, qw, kw)


def _conv_silu(cur_ref, prev_ref, next_ref, cw_ref, cb_ref, has_prev, has_next):
    prev = jnp.where(has_prev, prev_ref[0], 0.0)
    nxt = jnp.where(has_next, next_ref[0], 0.0)
    xin = jnp.concatenate([prev, cur_ref[0], nxt], axis=0)
    acc = jnp.zeros((CHUNK, CONV_CH), F32) + cb_ref[...]
    for t in range(D_CONV):
        off = SUBLANES - CONV_PAD + t
        acc = acc + xin[off:off + CHUNK, :] * cw_ref[t:t + 1, :]
    return acc * _sigmoid(acc)


def _scan_chunk(direction, xc, dtraw, bias, alog, tri_incl, tri_mask, state_ref):
    dt_all = _softplus(dtraw + bias)
    dta = dt_all * (-jnp.exp(alog))
    dta_t = jnp.transpose(dta)[0:2 * SSD_HEADS, :]
    u_r = jnp.dot(dta_t, tri_incl, precision=HIGHEST, preferred_element_type=F32)
    u_c = jnp.dot(tri_mask, dta, precision=HIGHEST, preferred_element_type=F32)
    tot = jnp.sum(dta, axis=0, keepdims=True)
    xs = xc[:, :SSD_W]
    ys = []
    for g in range(SSD_GROUPS):
        bg = xc[:, SSD_W + g * D_STATE:SSD_W + (g + 1) * D_STATE]
        cg = xc[:, SSD_W + (SSD_GROUPS + g) * D_STATE:SSD_W + (SSD_GROUPS + g + 1) * D_STATE]
        cg16 = cg.astype(BF16)
        cb = _dot_nt(cg16, bg.astype(BF16))
        st = state_ref[direction, g]
        y_off = _dot(cg16, st.astype(BF16))
        xw, keep = [], []
        for hh in range(HEADS_PER_GROUP):
            h = g * HEADS_PER_GROUP + hh
            col = direction * SSD_HEADS + h
            uc = u_c[:, col:col + 1]
            decay = jnp.exp(jnp.where(tri_mask > 0.0, uc - u_r[col:col + 1, :], -jnp.inf))
            xdt = xs[:, h * SSD_HEAD_DIM:(h + 1) * SSD_HEAD_DIM] * dt_all[:, col:col + 1]
            y_diag = _dot((cb * decay).astype(BF16), xdt.astype(BF16))
            ys.append(y_diag + y_off[:, hh * SSD_HEAD_DIM:(hh + 1) * SSD_HEAD_DIM] * jnp.exp(uc))
            tot_h = tot[:, col:col + 1]
            xw.append((xdt * jnp.exp(tot_h - uc)).astype(BF16))
            keep.append(jnp.broadcast_to(jnp.exp(tot_h), (1, SSD_HEAD_DIM)))
        state_ref[direction, g] = (st * jnp.concatenate(keep, axis=-1)
                                   + _dot_tn(bg.astype(BF16), jnp.concatenate(xw, axis=-1)))
    return jnp.concatenate(ys, axis=-1)


def _gate_norm(y, z, w):
    y = y * (z * _sigmoid(z))
    ms = jnp.mean(y * y, axis=-1, keepdims=True)
    return (y * lax.rsqrt(ms + EPS) * w).astype(BF16)


def _ssd_kernel(nc, curf_ref, prevf_ref, nextf_ref, curb_ref, prevb_ref, nextb_ref, dtf_ref, dtb_ref,
                zf_ref, zb_ref, cw_ref, cb_ref, bias_ref, alog_ref, mlow_ref, mup_ref, dskip_ref, nw_ref,
                o_ref, xc_ref, yacc_ref, state_ref):
    c = pl.program_id(1)
    cbk = nc - 1 - c
    first_visit = c < nc // 2
    rows_f = pl.ds(pl.multiple_of(c * CHUNK, CHUNK), CHUNK)
    rows_b = pl.ds(pl.multiple_of(cbk * CHUNK, CHUNK), CHUNK)

    @pl.when(c == 0)
    def _():
        state_ref[...] = jnp.zeros_like(state_ref)

    @pl.when(first_visit)
    def _():
        xc_ref[rows_f, :] = _conv_silu(curf_ref, prevf_ref, nextf_ref, cw_ref, cb_ref, c > 0, c < nc - 1)
        xc_ref[rows_b, :] = _conv_silu(curb_ref, prevb_ref, nextb_ref, cw_ref, cb_ref, cbk > 0, cbk < nc - 1)

    xcf = xc_ref[rows_f, :]
    xcb = xc_ref[rows_b, :]
    mlow = mlow_ref[...]
    mup = mup_ref[...]
    yf = _scan_chunk(0, xcf, dtf_ref[0], bias_ref[...], alog_ref[...], mup, mlow, state_ref)
    yb = _scan_chunk(1, xcb, dtb_ref[0], bias_ref[...], alog_ref[...], mlow, mup, state_ref)
    yf = yf + dskip_ref[...] * xcf[:, :SSD_W]

    @pl.when(first_visit)
    def _():
        yacc_ref[rows_f, :] = yf
        yacc_ref[rows_b, :] = yb

    @pl.when(jnp.logical_not(first_visit))
    def _():
        o_ref[0, rows_f, :] = _gate_norm(yacc_ref[rows_f, :] + yf, zf_ref[0], nw_ref[...])
        o_ref[0, rows_b, :] = _gate_norm(yacc_ref[rows_b, :] + yb, zb_ref[0], nw_ref[...])


def _ssd(xbc, dt_raw, z, conv_w, conv_b, bias, alog, dskip, nw, b, s):
    nc = s // CHUNK
    assert nc % 2 == 0
    hb = CHUNK // SUBLANES
    nhb = s // SUBLANES
    half = nc // 2
    tri = jnp.tril(jnp.ones((CHUNK, CHUNK), F32))
    fpos = lambda c: jnp.minimum(c, half - 1)
    bpos = lambda c: jnp.maximum(nc - 1 - c, half)
    cur = lambda pos: pl.BlockSpec((1, CHUNK, CONV_CH), lambda bi, c: (bi, pos(c), 0))
    prev = lambda pos: pl.BlockSpec((1, SUBLANES, CONV_CH),
                                    lambda bi, c: (bi, jnp.maximum(pos(c) * hb - 1, 0), 0))
    nxt = lambda pos: pl.BlockSpec((1, SUBLANES, CONV_CH),
                                   lambda bi, c: (bi, jnp.minimum((pos(c) + 1) * hb, nhb - 1), 0))
    chunk_f = lambda n: pl.BlockSpec((1, CHUNK, n), lambda bi, c: (bi, c, 0))
    chunk_b = lambda n: pl.BlockSpec((1, CHUNK, n), lambda bi, c: (bi, nc - 1 - c, 0))
    return pl.pallas_call(
        functools.partial(_ssd_kernel, nc),
        grid=(b, nc),
        in_specs=[cur(fpos), prev(fpos), nxt(fpos), cur(bpos), prev(bpos), nxt(bpos),
                  chunk_f(LANES), chunk_b(LANES), chunk_f(SSD_W), chunk_b(SSD_W),
                  _full(conv_w.shape), _full(conv_b.shape), _full(bias.shape), _full(alog.shape),
                  _full(tri.shape), _full(tri.shape), _full(dskip.shape), _full(nw.shape)],
        out_specs=pl.BlockSpec((1, s, SSD_W), lambda bi, c: (bi, 0, 0)),
        out_shape=jax.ShapeDtypeStruct((b, s, SSD_W), BF16),
        scratch_shapes=[pltpu.VMEM((s, CONV_CH), F32),
                        pltpu.VMEM((s, SSD_W), F32),
                        pltpu.VMEM((2, SSD_GROUPS, D_STATE, GROUP_W), F32)],
        compiler_params=_params(2),
        name="ssd",
    )(xbc, xbc, xbc, xbc, xbc, xbc, dt_raw, dt_raw, z, z, conv_w, conv_b, bias, alog, tri, tri.T, dskip, nw)


def _out_proj_kernel(x_ref, ssd_ref, attn_ref, g_ref, wso_ref, wao_ref, wout_ref,
                     fnw_ref, wr_ref, br_ref, x2_ref, h2_ref, comb_ref):
    g = g_ref[...]
    dm = x_ref.shape[1]
    merged = g[:, :dm] * _dot(attn_ref[...], wao_ref[...]) + g[:, dm:] * _dot(ssd_ref[...], wso_ref[...])
    x2 = x_ref[...] + _dot(merged.astype(BF16), wout_ref[...])
    x2_ref[...] = x2

    ms2 = jnp.mean(x2 * x2, axis=-1, keepdims=True)
    h2 = x2 * lax.rsqrt(ms2 + EPS) * fnw_ref[...]
    h2_hi = h2.astype(BF16)
    h2_ref[...] = h2_hi

    h2_lo = (h2 - h2_hi.astype(F32)).astype(BF16)
    pp = _dot(h2_hi, wr_ref[...]) + _dot(h2_lo, wr_ref[...])
    lg = pp[:, :LANES] + pp[:, LANES:] + br_ref[...]
    lane = lax.broadcasted_iota(jnp.int32, lg.shape, 1)
    neg = -jnp.inf
    big = jnp.int32(LANES)
    is_g = (lane >= N_EXPERTS) & (lane < N_EXPERTS + N_EXPERT_GROUPS)
    gl = jnp.where(is_g, lg, neg)
    ge = jnp.exp(gl - jnp.max(gl, axis=-1, keepdims=True))
    pg = ge / jnp.sum(ge, axis=-1, keepdims=True)
    g_val = jnp.max(pg, axis=-1, keepdims=True)
    g_idx = jnp.min(jnp.where(is_g & (pg == g_val), lane, big), axis=-1, keepdims=True) - N_EXPERTS
    lo = g_idx * EXPERTS_PER_GROUP
    sel = (lane >= lo) & (lane < lo + EXPERTS_PER_GROUP)
    fl = jnp.where(sel, lg, neg)
    fe = jnp.exp(fl - jnp.max(fl, axis=-1, keepdims=True))
    pf = fe / jnp.sum(fe, axis=-1, keepdims=True)
    v1 = jnp.max(pf, axis=-1, keepdims=True)
    i1 = jnp.min(jnp.where(sel & (pf == v1), lane, big), axis=-1, keepdims=True)
    pf2 = jnp.where(sel & (lane != i1), pf, -1.0)
    v2 = jnp.max(pf2, axis=-1, keepdims=True)
    i2 = jnp.min(jnp.where(pf2 == v2, lane, big), axis=-1, keepdims=True)
    den = v1 + v2
    comb_ref[...] = g_val * jnp.where(lane == i1, v1 / den, jnp.where(lane == i2, v2 / den, 0.0))


def _out_proj(x2d, ssd, attn, g, wso, wao, wout, fnw, wr, br):
    t, d = x2d.shape
    tm = TM_PROJ
    row = lambda n: pl.BlockSpec((tm, n), lambda i: (i, 0))
    return pl.pallas_call(
        _out_proj_kernel,
        grid=(t // tm,),
        in_specs=[row(d), row(SSD_W), row(ATTN_W), row(2 * d),
                  _full(wso.shape), _full(wao.shape), _full(wout.shape),
                  _full(fnw.shape), _full(wr.shape), _full(br.shape)],
        out_specs=[row(d), row(d), row(LANES)],
        out_shape=[jax.ShapeDtypeStruct((t, d), F32),
                   jax.ShapeDtypeStruct((t, d), BF16),
                   jax.ShapeDtypeStruct((t, LANES), F32)],
        compiler_params=_params(1),
        name="out_proj",
    )(x2d, ssd, attn, g, wso, wao, wout, fnw, wr, br)


def _moe_kernel(x2_ref, h2_ref, comb_ref, w1_ref, w3_ref, w2_ref, o_ref):
    e = pl.program_id(1)

    @pl.when(e == 0)
    def _():
        o_ref[...] = x2_ref[...]

    h = h2_ref[...]
    a = _dot(h, w1_ref[0])
    act = (a * _sigmoid(a)) * _dot(h, w3_ref[0])
    comb = comb_ref[...]
    lane = lax.broadcasted_iota(jnp.int32, comb.shape, 1)
    ce = jnp.sum(jnp.where(lane == e, comb, 0.0), axis=-1, keepdims=True)
    o_ref[...] += ce * _dot(act.astype(BF16), w2_ref[0])


def _moe(x2, h2, comb, w1, w3, w2):
    t, d = x2.shape
    tm = TM_MOE
    de = w1.shape[2]
    row = lambda n: pl.BlockSpec((tm, n), lambda i, e: (i, 0))
    return pl.pallas_call(
        _moe_kernel,
        grid=(t // tm, N_EXPERTS),
        in_specs=[row(d), row(d), row(LANES),
                  pl.BlockSpec((1, d, de), lambda i, e: (e, 0, 0)),
                  pl.BlockSpec((1, d, de), lambda i, e: (e, 0, 0)),
                  pl.BlockSpec((1, de, d), lambda i, e: (e, 0, 0))],
        out_specs=row(d),
        out_shape=jax.ShapeDtypeStruct((t, d), F32),
        compiler_params=_params(2),
        name="moe",
    )(x2, h2, comb, w1, w3, w2)


def _rope_tables(s):
    rows = s // GRID_W
    row = jnp.repeat(jnp.arange(rows, dtype=jnp.int32), GRID_W)
    col = jnp.tile(jnp.arange(GRID_W, dtype=jnp.int32), rows)
    half = HEAD_DIM // 2
    inv_freq = ROPE_THETA ** (-jnp.arange(0, half, 2, dtype=F32) / half)
    ang_r = row.astype(F32)[:, None] * inv_freq[None, :]
    ang_c = col.astype(F32)[:, None] * inv_freq[None, :]
    cos = jnp.concatenate([jnp.cos(ang_r), jnp.cos(ang_r), jnp.cos(ang_c), jnp.cos(ang_c)], axis=-1)
    sin_signed = jnp.concatenate([-jnp.sin(ang_r), jnp.sin(ang_r), -jnp.sin(ang_c), jnp.sin(ang_c)], axis=-1)
    return cos, sin_signed


def _pad_lanes(v):
    return jnp.pad(v, (0, LANES - v.shape[0]))[None]


def kernel(x, norm_mix_w, w_in, b_gate, q_norm_w, k_norm_w, w_attn_o, conv_w, conv_b, dt_bias, a_log, d_skip,
           ssd_norm_w, w_ssd_o, w_out, norm_ffn_w, w_router_group, b_router_group, w_router_expert,
           b_router_expert, w1, w3, w2):
    b, s, d = x.shape
    t = b * s
    depth = norm_mix_w.shape[0]
    cos, sin_signed = _rope_tables(s)
    x2d = x.reshape(t, d)
    sizes = (ATTN_W, KV_W, KV_W, SSD_W, CONV_CH, 2 * SSD_HEADS, 2 * d)
    offs = [0]
    for n in sizes:
        offs.append(offs[-1] + n)

    for l in range(depth):
        wi = w_in[l].astype(BF16)
        wq, wk, wv, wz, wxbc, wdt, wg = (wi[:, offs[j]:offs[j + 1]] for j in range(7))
        wdt = jnp.pad(wdt, ((0, 0), (0, LANES - 2 * SSD_HEADS)))
        q, k, v, z, xbc, dt_raw, g = _in_proj(x2d, norm_mix_w[l][None], wq, wk, wv, wz, wxbc, wdt, wg,
                                              b_gate[l][None])

        attn = _attention(q.reshape(b, s, ATTN_W), k, v, cos, sin_signed, q_norm_w[l][None], k_norm_w[l][None],
                          b, s)

        cw = jnp.pad(conv_w[l], ((0, SUBLANES - D_CONV), (0, 0)))
        ssd = _ssd(xbc.reshape(b, s, CONV_CH), dt_raw.reshape(b, s, LANES), z.reshape(b, s, SSD_W),
                   cw, conv_b[l][None], _pad_lanes(dt_bias[l].reshape(-1)), _pad_lanes(a_log[l].reshape(-1)),
                   jnp.repeat(d_skip[l], SSD_HEAD_DIM)[None], ssd_norm_w[l][None], b, s)

        wr = jnp.concatenate([w_router_expert[l], w_router_group[l]], axis=1)
        wr = jnp.pad(wr, ((0, 0), (0, LANES - wr.shape[1])))
        wr_hi = wr.astype(BF16)
        wr_lo = (wr - wr_hi.astype(F32)).astype(BF16)
        br = _pad_lanes(jnp.concatenate([b_router_expert[l], b_router_group[l]]))
        x2, h2, comb = _out_proj(x2d, ssd.reshape(t, SSD_W), attn.reshape(t, ATTN_W), g,
                                 w_ssd_o[l].astype(BF16), w_attn_o[l].astype(BF16), w_out[l].astype(BF16),
                                 norm_ffn_w[l][None], jnp.concatenate([wr_hi, wr_lo], axis=1), br)

        x2d = _moe(x2, h2, comb, w1[l].astype(BF16), w3[l].astype(BF16), w2[l].astype(BF16))
    return x2d.reshape(b, s, d)
```

```python
import functools
import math

import jax
import jax.numpy as jnp
from jax import lax
from jax.experimental import pallas as pl
from jax.experimental.pallas import tpu as pltpu
from jax.experimental.pallas import tpu_sc as plsc

GRID_W = 64
HEAD_DIM = 64
N_Q_HEADS = 8
N_KV_HEADS = 2
GQA_GROUP = N_Q_HEADS // N_KV_HEADS
ATTN_W = N_Q_HEADS * HEAD_DIM
KV_W = N_KV_HEADS * HEAD_DIM
ROPE_THETA = 10000.0
SSD_HEAD_DIM = 64
SSD_HEADS = 8
SSD_W = SSD_HEADS * SSD_HEAD_DIM
SSD_GROUPS = 2
HEADS_PER_GROUP = SSD_HEADS // SSD_GROUPS
GROUP_W = HEADS_PER_GROUP * SSD_HEAD_DIM
D_STATE = 64
D_CONV = 7
CONV_PAD = (D_CONV - 1) // 2
CONV_CH = SSD_W + 2 * SSD_GROUPS * D_STATE
CHUNK = 128
N_EXPERT_GROUPS = 4
EXPERTS_PER_GROUP = 4
N_EXPERTS = N_EXPERT_GROUPS * EXPERTS_PER_GROUP
EPS = 1e-6

LANES = 128
SUBLANES = 8
VMEM_LIMIT = 56 * 1024 * 1024

SC_WINDOW = 128
SC_ROW_WORDS = 256

TM_PROJ = 256
TQ = 256
TM_MOE = 512

F32 = jnp.float32
BF16 = jnp.bfloat16
HIGHEST = lax.Precision.HIGHEST


def _dot(a, b):
    return jnp.dot(a, b, preferred_element_type=F32)


def _dot_nt(a, b):
    return lax.dot_general(a, b, (((1,), (1,)), ((), ())), preferred_element_type=F32)


def _dot_tn(a, b):
    return lax.dot_general(a, b, (((0,), (0,)), ((), ())), preferred_element_type=F32)


def _sigmoid(x):
    return 1.0 / (1.0 + jnp.exp(-x))


def _softplus(x):
    return jnp.maximum(x, 0.0) + jnp.log1p(jnp.exp(-jnp.abs(x)))


def _params(n_axes):
    return pltpu.CompilerParams(dimension_semantics=("arbitrary",) * n_axes,
                                vmem_limit_bytes=VMEM_LIMIT)


def _full(shape):
    return pl.BlockSpec(shape, lambda *_: (0,) * len(shape))


def _in_proj_kernel(x_ref, nw_ref, wq_ref, wk_ref, wv_ref, wz_ref, wxbc_ref, wdt_ref, wg_ref, bg_ref,
                    q_ref, k_ref, v_ref, z_ref, xbc_ref, dt_ref, g_ref):
    x = x_ref[...]
    ms = jnp.mean(x * x, axis=-1, keepdims=True)
    h = (x * lax.rsqrt(ms + EPS) * nw_ref[...]).astype(BF16)
    q_ref[...] = _dot(h, wq_ref[...])
    k = _dot(h, wk_ref[...])
    v = _dot(h, wv_ref[...]).astype(BF16)
    for j in range(N_KV_HEADS):
        k_ref[j] = k[:, j * HEAD_DIM:(j + 1) * HEAD_DIM]
        v_ref[j] = v[:, j * HEAD_DIM:(j + 1) * HEAD_DIM]
    z_ref[...] = _dot(h, wz_ref[...])
    xbc_ref[...] = _dot(h, wxbc_ref[...])
    dt_ref[...] = _dot(h, wdt_ref[...])
    g_ref[...] = _sigmoid(_dot(h, wg_ref[...]) + bg_ref[...]).astype(BF16)


def _in_proj(x2d, nw, wq, wk, wv, wz, wxbc, wdt, wg, bg):
    t, d = x2d.shape
    tm = TM_PROJ
    row = lambda n: pl.BlockSpec((tm, n), lambda i: (i, 0))
    return pl.pallas_call(
        _in_proj_kernel,
        grid=(t // tm,),
        in_specs=[row(d), _full(nw.shape), _full(wq.shape), _full(wk.shape), _full(wv.shape),
                  _full(wz.shape), _full(wxbc.shape), _full(wdt.shape), _full(wg.shape), _full(bg.shape)],
        out_specs=[row(ATTN_W),
                   pl.BlockSpec((N_KV_HEADS, tm, HEAD_DIM), lambda i: (0, i, 0)),
                   pl.BlockSpec((N_KV_HEADS, tm, HEAD_DIM), lambda i: (0, i, 0)),
                   row(SSD_W), row(CONV_CH), row(LANES), row(wg.shape[1])],
        out_shape=[jax.ShapeDtypeStruct((t, ATTN_W), F32),
                   jax.ShapeDtypeStruct((N_KV_HEADS, t, HEAD_DIM), F32),
                   jax.ShapeDtypeStruct((N_KV_HEADS, t, HEAD_DIM), BF16),
                   jax.ShapeDtypeStruct((t, SSD_W), F32),
                   jax.ShapeDtypeStruct((t, CONV_CH), F32),
                   jax.ShapeDtypeStruct((t, LANES), F32),
                   jax.ShapeDtypeStruct((t, wg.shape[1]), BF16)],
        compiler_params=_params(1),
        name="in_proj",
    )(x2d, nw, wq, wk, wv, wz, wxbc, wdt, wg, bg)


def _rope_partner(u):
    q = HEAD_DIM // 4
    return jnp.concatenate([u[:, q:2 * q], u[:, 0:q], u[:, 3 * q:4 * q], u[:, 2 * q:3 * q]], axis=-1)


def _norm_rope(u, w, cos, sin_signed):
    ms = jnp.mean(u * u, axis=-1, keepdims=True)
    un = u * lax.rsqrt(ms + EPS) * w
    return un * cos + _rope_partner(un) * sin_signed


def _attn_kernel(q_ref, k_ref, v_ref, cosq_ref, sinq_ref, cosk_ref, sink_ref, qw_ref, kw_ref,
                 o_ref, ks_ref):
    @pl.when(pl.program_id(2) == 0)
    def _():
        ks_ref[...] = _norm_rope(k_ref[0], kw_ref[...], cosk_ref[...], sink_ref[...]).astype(BF16)

    q = q_ref[0]
    cos = cosq_ref[...]
    sin = sinq_ref[...]
    scale = 1.0 / math.sqrt(HEAD_DIM)
    outs = []
    for g in range(GQA_GROUP):
        qg = _norm_rope(q[:, g * HEAD_DIM:(g + 1) * HEAD_DIM], qw_ref[...], cos, sin) * scale
        s = _dot_nt(qg.astype(BF16), ks_ref[...])
        m = jnp.max(s, axis=-1, keepdims=True)
        p = jnp.exp(s - m)
        l = jnp.sum(p, axis=-1, keepdims=True)
        outs.append(_dot(p.astype(BF16), v_ref[0]) / l)
    o_ref[0] = jnp.concatenate(outs, axis=-1).astype(BF16)


def _attention(q, k, v, cos, sin_signed, qw, kw, b, s):
    tq = TQ
    gw = GQA_GROUP * HEAD_DIM
    return pl.pallas_call(
        _attn_kernel,
        grid=(b, N_KV_HEADS, s // tq),
        in_specs=[pl.BlockSpec((1, tq, gw), lambda bi, kv, qi: (bi, qi, kv)),
                  pl.BlockSpec((1, s, HEAD_DIM), lambda bi, kv, qi: (kv, bi, 0)),
                  pl.BlockSpec((1, s, HEAD_DIM), lambda bi, kv, qi: (kv, bi, 0)),
                  pl.BlockSpec((tq, HEAD_DIM), lambda bi, kv, qi: (qi, 0)),
                  pl.BlockSpec((tq, HEAD_DIM), lambda bi, kv, qi: (qi, 0)),
                  _full((s, HEAD_DIM)), _full((s, HEAD_DIM)),
                  _full((1, HEAD_DIM)), _full((1, HEAD_DIM))],
        out_specs=pl.BlockSpec((1, tq, gw), lambda bi, kv, qi: (bi, qi, kv)),
        out_shape=jax.ShapeDtypeStruct((b, s, ATTN_W), BF16),
        scratch_shapes=[pltpu.VMEM((s, HEAD_DIM), BF16)],
        compiler_params=_params(3),
        name="attention",
    )(q, k, v, cos, sin_signed, cos, sin_signed, qw, kw)


def _conv_silu(cur_ref, prev_ref, next_ref, cw_ref, cb_ref, has_prev, has_next):
    prev = jnp.where(has_prev, prev_ref[0], 0.0)
    nxt = jnp.where(has_next, next_ref[0], 0.0)
    xin = jnp.concatenate([prev, cur_ref[0], nxt], axis=0)
    acc = jnp.zeros((CHUNK, CONV_CH), F32) + cb_ref[...]
    for t in range(D_CONV):
        off = SUBLANES - CONV_PAD + t
        acc = acc + xin[off:off + CHUNK, :] * cw_ref[t:t + 1, :]
    return acc * _sigmoid(acc)


def _scan_chunk(direction, xc, dtraw, bias, alog, tri_incl, tri_mask, state_ref):
    dt_all = _softplus(dtraw + bias)
    dta = dt_all * (-jnp.exp(alog))
    dta_t = jnp.transpose(dta)[0:2 * SSD_HEADS, :]
    u_r = jnp.dot(dta_t, tri_incl, precision=HIGHEST, preferred_element_type=F32)
    u_c = jnp.dot(tri_mask, dta, precision=HIGHEST, preferred_element_type=F32)
    tot = jnp.sum(dta, axis=0, keepdims=True)
    xs = xc[:, :SSD_W]
    ys = []
    for g in range(SSD_GROUPS):
        bg = xc[:, SSD_W + g * D_STATE:SSD_W + (g + 1) * D_STATE]
        cg = xc[:, SSD_W + (SSD_GROUPS + g) * D_STATE:SSD_W + (SSD_GROUPS + g + 1) * D_STATE]
        cg16 = cg.astype(BF16)
        cb = _dot_nt(cg16, bg.astype(BF16))
        st = state_ref[direction, g]
        y_off = _dot(cg16, st.astype(BF16))
        xw, keep = [], []
        for hh in range(HEADS_PER_GROUP):
            h = g * HEADS_PER_GROUP + hh
            col = direction * SSD_HEADS + h
            uc = u_c[:, col:col + 1]
            decay = jnp.exp(jnp.where(tri_mask > 0.0, uc - u_r[col:col + 1, :], -jnp.inf))
            xdt = xs[:, h * SSD_HEAD_DIM:(h + 1) * SSD_HEAD_DIM] * dt_all[:, col:col + 1]
            y_diag = _dot((cb * decay).astype(BF16), xdt.astype(BF16))
            ys.append(y_diag + y_off[:, hh * SSD_HEAD_DIM:(hh + 1) * SSD_HEAD_DIM] * jnp.exp(uc))
            tot_h = tot[:, col:col + 1]
            xw.append((xdt * jnp.exp(tot_h - uc)).astype(BF16))
            keep.append(jnp.broadcast_to(jnp.exp(tot_h), (1, SSD_HEAD_DIM)))
        state_ref[direction, g] = (st * jnp.concatenate(keep, axis=-1)
                                   + _dot_tn(bg.astype(BF16), jnp.concatenate(xw, axis=-1)))
    return jnp.concatenate(ys, axis=-1)


def _gate_norm(y, z, w):
    y = y * (z * _sigmoid(z))
    ms = jnp.mean(y * y, axis=-1, keepdims=True)
    return (y * lax.rsqrt(ms + EPS) * w).astype(BF16)


def _ssd_kernel(nc, curf_ref, prevf_ref, nextf_ref, curb_ref, prevb_ref, nextb_ref, dtf_ref, dtb_ref,
                zf_ref, zb_ref, cw_ref, cb_ref, bias_ref, alog_ref, mlow_ref, mup_ref, dskip_ref, nw_ref,
                o_ref, xc_ref, yacc_ref, state_ref):
    c = pl.program_id(1)
    cbk = nc - 1 - c
    first_visit = c < nc // 2
    rows_f = pl.ds(pl.multiple_of(c * CHUNK, CHUNK), CHUNK)
    rows_b = pl.ds(pl.multiple_of(cbk * CHUNK, CHUNK), CHUNK)

    @pl.when(c == 0)
    def _():
        state_ref[...] = jnp.zeros_like(state_ref)

    @pl.when(first_visit)
    def _():
        xc_ref[rows_f, :] = _conv_silu(curf_ref, prevf_ref, nextf_ref, cw_ref, cb_ref, c > 0, c < nc - 1)
        xc_ref[rows_b, :] = _conv_silu(curb_ref, prevb_ref, nextb_ref, cw_ref, cb_ref, cbk > 0, cbk < nc - 1)

    xcf = xc_ref[rows_f, :]
    xcb = xc_ref[rows_b, :]
    mlow = mlow_ref[...]
    mup = mup_ref[...]
    yf = _scan_chunk(0, xcf, dtf_ref[0], bias_ref[...], alog_ref[...], mup, mlow, state_ref)
    yb = _scan_chunk(1, xcb, dtb_ref[0], bias_ref[...], alog_ref[...], mlow, mup, state_ref)
    yf = yf + dskip_ref[...] * xcf[:, :SSD_W]

    @pl.when(first_visit)
    def _():
        yacc_ref[rows_f, :] = yf
        yacc_ref[rows_b, :] = yb

    @pl.when(jnp.logical_not(first_visit))
    def _():
        o_ref[0, rows_f, :] = _gate_norm(yacc_ref[rows_f, :] + yf, zf_ref[0], nw_ref[...])
        o_ref[0, rows_b, :] = _gate_norm(yacc_ref[rows_b, :] + yb, zb_ref[0], nw_ref[...])


def _ssd(xbc, dt_raw, z, conv_w, conv_b, bias, alog, dskip, nw, b, s):
    nc = s // CHUNK
    assert nc % 2 == 0
    hb = CHUNK // SUBLANES
    nhb = s // SUBLANES
    half = nc // 2
    tri = jnp.tril(jnp.ones((CHUNK, CHUNK), F32))
    fpos = lambda c: jnp.minimum(c, half - 1)
    bpos = lambda c: jnp.maximum(nc - 1 - c, half)
    cur = lambda pos: pl.BlockSpec((1, CHUNK, CONV_CH), lambda bi, c: (bi, pos(c), 0))
    prev = lambda pos: pl.BlockSpec((1, SUBLANES, CONV_CH),
                                    lambda bi, c: (bi, jnp.maximum(pos(c) * hb - 1, 0), 0))
    nxt = lambda pos: pl.BlockSpec((1, SUBLANES, CONV_CH),
                                   lambda bi, c: (bi, jnp.minimum((pos(c) + 1) * hb, nhb - 1), 0))
    chunk_f = lambda n: pl.BlockSpec((1, CHUNK, n), lambda bi, c: (bi, c, 0))
    chunk_b = lambda n: pl.BlockSpec((1, CHUNK, n), lambda bi, c: (bi, nc - 1 - c, 0))
    return pl.pallas_call(
        functools.partial(_ssd_kernel, nc),
        grid=(b, nc),
        in_specs=[cur(fpos), prev(fpos), nxt(fpos), cur(bpos), prev(bpos), nxt(bpos),
                  chunk_f(LANES), chunk_b(LANES), chunk_f(SSD_W), chunk_b(SSD_W),
                  _full(conv_w.shape), _full(conv_b.shape), _full(bias.shape), _full(alog.shape),
                  _full(tri.shape), _full(tri.shape), _full(dskip.shape), _full(nw.shape)],
        out_specs=pl.BlockSpec((1, s, SSD_W), lambda bi, c: (bi, 0, 0)),
        out_shape=jax.ShapeDtypeStruct((b, s, SSD_W), BF16),
        scratch_shapes=[pltpu.VMEM((s, CONV_CH), F32),
                        pltpu.VMEM((s, SSD_W), F32),
                        pltpu.VMEM((2, SSD_GROUPS, D_STATE, GROUP_W), F32)],
        compiler_params=_params(2),
        name="ssd",
    )(xbc, xbc, xbc, xbc, xbc, xbc, dt_raw, dt_raw, z, z, conv_w, conv_b, bias, alog, tri, tri.T, dskip, nw)


def _pack_bf16_pairs(v):
    n = v.shape[1] // 2
    hi = pltpu.bitcast(v[:, :n].astype(BF16).astype(F32), jnp.uint32)
    lo = pltpu.bitcast(v[:, n:].astype(BF16).astype(F32), jnp.uint32)
    return hi | (lo >> 16)


def _unpack_bf16_pairs(u):
    hi = pltpu.bitcast(u & jnp.uint32(0xFFFF0000), F32)
    lo = pltpu.bitcast(u << 16, F32)
    return hi, lo


R_E1, R_E2, R_C1, R_C2, R_RANK1, R_RANK2 = range(6)


def _out_proj_kernel(x_ref, ssd_ref, attn_ref, g_ref, wso_ref, wao_ref, wout_ref,
                     fnw_ref, wr_ref, br_ref, tril_ref, x2_ref, h2_ref, route_ref, cnt_ref, base_ref):
    @pl.when(pl.program_id(0) == 0)
    def _():
        base_ref[...] = jnp.zeros_like(base_ref)

    g = g_ref[...]
    dm = x_ref.shape[1]
    merged = g[:, :dm] * _dot(attn_ref[...], wao_ref[...]) + g[:, dm:] * _dot(ssd_ref[...], wso_ref[...])
    x2 = x_ref[...] + _dot(merged.astype(BF16), wout_ref[...])
    x2_ref[...] = x2

    ms2 = jnp.mean(x2 * x2, axis=-1, keepdims=True)
    h2 = x2 * lax.rsqrt(ms2 + EPS) * fnw_ref[...]
    h2_hi = h2.astype(BF16)
    h2_ref[...] = _pack_bf16_pairs(h2)

    h2_lo = (h2 - h2_hi.astype(F32)).astype(BF16)
    pp = _dot(h2_hi, wr_ref[...]) + _dot(h2_lo, wr_ref[...])
    lg = pp[:, :LANES] + pp[:, LANES:] + br_ref[...]
    lane = lax.broadcasted_iota(jnp.int32, lg.shape, 1)
    neg = -jnp.inf
    big = jnp.int32(LANES)
    is_g = (lane >= N_EXPERTS) & (lane < N_EXPERTS + N_EXPERT_GROUPS)
    gl = jnp.where(is_g, lg, neg)
    ge = jnp.exp(gl - jnp.max(gl, axis=-1, keepdims=True))
    pg = ge / jnp.sum(ge, axis=-1, keepdims=True)
    g_val = jnp.max(pg, axis=-1, keepdims=True)
    g_idx = jnp.min(jnp.where(is_g & (pg == g_val), lane, big), axis=-1, keepdims=True) - N_EXPERTS
    lo = g_idx * EXPERTS_PER_GROUP
    sel = (lane >= lo) & (lane < lo + EXPERTS_PER_GROUP)
    fl = jnp.where(sel, lg, neg)
    fe = jnp.exp(fl - jnp.max(fl, axis=-1, keepdims=True))
    pf = fe / jnp.sum(fe, axis=-1, keepdims=True)
    v1 = jnp.max(pf, axis=-1, keepdims=True)
    i1 = jnp.min(jnp.where(sel & (pf == v1), lane, big), axis=-1, keepdims=True)
    pf2 = jnp.where(sel & (lane != i1), pf, -1.0)
    v2 = jnp.max(pf2, axis=-1, keepdims=True)
    i2 = jnp.min(jnp.where(pf2 == v2, lane, big), axis=-1, keepdims=True)
    den = v1 + v2
    c1 = g_val * (v1 / den)
    c2 = g_val * (v2 / den)

    oh = jnp.where(lane == i1, 1.0, 0.0) + jnp.where(lane == i2, 1.0, 0.0)
    cnt = _dot(tril_ref[...], oh.astype(BF16)) + base_ref[...]
    r1 = jnp.sum(jnp.where(lane == i1, cnt, 0.0), axis=-1, keepdims=True)
    r2 = jnp.sum(jnp.where(lane == i2, cnt, 0.0), axis=-1, keepdims=True)
    base = base_ref[...] + jnp.sum(oh, axis=0, keepdims=True)
    base_ref[...] = base
    cnt_ref[...] = jnp.broadcast_to(base, cnt_ref.shape)

    rec = jnp.zeros(lg.shape, F32)
    for slot, val in ((R_E1, i1.astype(F32)), (R_E2, i2.astype(F32)), (R_C1, c1), (R_C2, c2),
                      (R_RANK1, r1), (R_RANK2, r2)):
        rec = jnp.where(lane == slot, val, rec)
    route_ref[...] = rec


def _out_proj(x2d, ssd, attn, g, wso, wao, wout, fnw, wr, br):
    t, d = x2d.shape
    tm = TM_PROJ
    row = lambda n: pl.BlockSpec((tm, n), lambda i: (i, 0))
    tril_strict = jnp.tril(jnp.ones((tm, tm), BF16), k=-1)
    return pl.pallas_call(
        _out_proj_kernel,
        grid=(t // tm,),
        in_specs=[row(d), row(SSD_W), row(ATTN_W), row(2 * d),
                  _full(wso.shape), _full(wao.shape), _full(wout.shape),
                  _full(fnw.shape), _full(wr.shape), _full(br.shape), _full(tril_strict.shape)],
        out_specs=[row(d), row(d // 2), row(LANES), _full((SUBLANES, LANES))],
        out_shape=[jax.ShapeDtypeStruct((t, d), F32),
                   jax.ShapeDtypeStruct((t, d // 2), jnp.uint32),
                   jax.ShapeDtypeStruct((t, LANES), F32),
                   jax.ShapeDtypeStruct((SUBLANES, LANES), F32)],
        scratch_shapes=[pltpu.VMEM((1, LANES), F32)],
        compiler_params=_params(1),
        name="out_proj",
    )(x2d, ssd, attn, g, wso, wao, wout, fnw, wr, br, tril_strict)


def _sc_mesh():
    return plsc.VectorSubcoreMesh(core_axis_name="core", subcore_axis_name="subcore")


def _sc_scatter_rows(rows, idx_a, idx_b, n_out):
    n, w = rows.shape

    @pl.kernel(out_type=jax.ShapeDtypeStruct((n_out, w), rows.dtype), mesh=_sc_mesh(), name="moe_scatter")
    def scatter(x_hbm, ia_hbm, ib_hbm, o_hbm):
        def body(x_vmem, ia_vmem, ib_vmem):
            pltpu.sync_copy(x_vmem, o_hbm.at[ia_vmem.at[0]])
            pltpu.sync_copy(x_vmem, o_hbm.at[ib_vmem.at[0]])

        pltpu.emit_pipeline(
            body, grid=(n // SC_WINDOW,),
            in_specs=[pl.BlockSpec((SC_WINDOW, w), lambda i: (i, 0)),
                      pl.BlockSpec((1, SC_WINDOW), lambda i: (0, i)),
                      pl.BlockSpec((1, SC_WINDOW), lambda i: (0, i))],
            out_specs=[],
            core_axis_name=("core", "subcore"),
            dimension_semantics=(pltpu.PARALLEL,),
        )(x_hbm, ia_hbm, ib_hbm)

    return scatter(rows, idx_a.reshape(1, n), idx_b.reshape(1, n))


def _sc_gather_rows(table, idx):
    m = idx.shape[0]
    w = table.shape[1]

    @pl.kernel(out_type=jax.ShapeDtypeStruct((m, w), table.dtype), mesh=_sc_mesh(), name="moe_gather")
    def gather(x_hbm, i_hbm, o_hbm):
        def body(i_vmem, o_vmem):
            pltpu.sync_copy(x_hbm.at[i_vmem.at[0]], o_vmem)

        pltpu.emit_pipeline(
            body, grid=(m // SC_WINDOW,),
            in_specs=[pl.BlockSpec((1, SC_WINDOW), lambda i: (0, i))],
            out_specs=[pl.BlockSpec((SC_WINDOW, w), lambda i: (i, 0))],
            core_axis_name=("core", "subcore"),
            dimension_semantics=(pltpu.PARALLEL,),
        )(i_hbm, o_hbm)

    return gather(table, idx.reshape(1, m))


def _gmm_kernel(te_ref, nt_ref, xs_ref, w1_ref, w3_ref, w2_ref, y_ref):
    @pl.when(pl.program_id(0) < nt_ref[0])
    def _():
        xa, xb = _unpack_bf16_pairs(xs_ref[...])
        xa = xa.astype(BF16)
        xb = xb.astype(BF16)
        half = xs_ref.shape[1]
        a = _dot(xa, w1_ref[0, :half, :]) + _dot(xb, w1_ref[0, half:, :])
        b = _dot(xa, w3_ref[0, :half, :]) + _dot(xb, w3_ref[0, half:, :])
        act = ((a * _sigmoid(a)) * b).astype(BF16)
        y_ref[...] = _pack_bf16_pairs(_dot(act, w2_ref[0]))


def _gmm(tile_expert, n_tiles, xs, w1, w3, w2):
    p, half = xs.shape
    tm = TM_MOE
    d, de = w1.shape[1], w1.shape[2]
    row = pl.BlockSpec((tm, half), lambda i, te, nt: (jnp.minimum(i, nt[0] - 1), 0))
    return pl.pallas_call(
        _gmm_kernel,
        grid_spec=pltpu.PrefetchScalarGridSpec(
            num_scalar_prefetch=2,
            grid=(p // tm,),
            in_specs=[row,
                      pl.BlockSpec((1, d, de), lambda i, te, nt: (te[i], 0, 0)),
                      pl.BlockSpec((1, d, de), lambda i, te, nt: (te[i], 0, 0)),
                      pl.BlockSpec((1, de, d), lambda i, te, nt: (te[i], 0, 0))],
            out_specs=row),
        out_shape=jax.ShapeDtypeStruct((p, half), jnp.uint32),
        compiler_params=_params(1),
        name="moe_gmm",
    )(tile_expert, n_tiles, xs, w1, w3, w2)


def _combine_kernel(x2_ref, ya_ref, yb_ref, route_ref, o_ref):
    route = route_ref[...]
    c1 = route[:, R_C1:R_C1 + 1]
    c2 = route[:, R_C2:R_C2 + 1]
    a_hi, a_lo = _unpack_bf16_pairs(ya_ref[...])
    b_hi, b_lo = _unpack_bf16_pairs(yb_ref[...])
    half = ya_ref.shape[1]
    o_ref[:, :half] = x2_ref[:, :half] + (c1 * a_hi + c2 * b_hi)
    o_ref[:, half:] = x2_ref[:, half:] + (c1 * a_lo + c2 * b_lo)


def _combine(x2, y12, route):
    t, d = x2.shape
    tm = TM_MOE
    nb = t // tm
    return pl.pallas_call(
        _combine_kernel,
        grid=(nb,),
        in_specs=[pl.BlockSpec((tm, d), lambda i: (i, 0)),
                  pl.BlockSpec((tm, d // 2), lambda i: (i, 0)),
                  pl.BlockSpec((tm, d // 2), lambda i: (i + nb, 0)),
                  pl.BlockSpec((tm, LANES), lambda i: (i, 0))],
        out_specs=pl.BlockSpec((tm, d), lambda i: (i, 0)),
        out_shape=jax.ShapeDtypeStruct((t, d), F32),
        compiler_params=_params(1),
        name="moe_combine",
    )(x2, y12, y12, route)


def _routed_moe(x2, h2p, route, counts, w1, w3, w2):
    t, half = h2p.shape
    tm = TM_MOE
    p_max = 2 * t + N_EXPERTS * tm
    e12 = route[:, R_E1:R_E2 + 1].astype(jnp.int32)
    r12 = route[:, R_RANK1:R_RANK2 + 1].astype(jnp.int32)
    cnt = counts[0, :N_EXPERTS].astype(jnp.int32)
    padded = (cnt + tm - 1) // tm * tm
    ends = jnp.cumsum(padded)
    pos = (ends - padded)[e12] + r12
    tile_ends = ends // tm
    tile_expert = jnp.minimum(jnp.searchsorted(tile_ends, jnp.arange(p_max // tm), side="right"),
                              N_EXPERTS - 1).astype(jnp.int32)
    n_tiles = tile_ends[-1:].astype(jnp.int32)

    split = half // SC_ROW_WORDS
    halves = lambda p: (p[:, None] * split + jnp.arange(split, dtype=jnp.int32)[None]).reshape(-1)
    xs = _sc_scatter_rows(h2p.reshape(t * split, SC_ROW_WORDS), halves(pos[:, 0]), halves(pos[:, 1]),
                          p_max * split).reshape(p_max, half)
    y = _gmm(tile_expert, n_tiles, xs, w1, w3, w2)
    y12 = _sc_gather_rows(y.reshape(p_max * split, SC_ROW_WORDS),
                          halves(jnp.concatenate([pos[:, 0], pos[:, 1]]))).reshape(2 * t, half)
    return _combine(x2, y12, route)


def _rope_tables(s):
    rows = s // GRID_W
    row = jnp.repeat(jnp.arange(rows, dtype=jnp.int32), GRID_W)
    col = jnp.tile(jnp.arange(GRID_W, dtype=jnp.int32), rows)
    half = HEAD_DIM // 2
    inv_freq = ROPE_THETA ** (-jnp.arange(0, half, 2, dtype=F32) / half)
    ang_r = row.astype(F32)[:, None] * inv_freq[None, :]
    ang_c = col.astype(F32)[:, None] * inv_freq[None, :]
    cos = jnp.concatenate([jnp.cos(ang_r), jnp.cos(ang_r), jnp.cos(ang_c), jnp.cos(ang_c)], axis=-1)
    sin_signed = jnp.concatenate([-jnp.sin(ang_r), jnp.sin(ang_r), -jnp.sin(ang_c), jnp.sin(ang_c)], axis=-1)
    return cos, sin_signed


def _pad_lanes(v):
    return jnp.pad(v, (0, LANES - v.shape[0]))[None]


def kernel(x, norm_mix_w, w_in, b_gate, q_norm_w, k_norm_w, w_attn_o, conv_w, conv_b, dt_bias, a_log, d_skip,
           ssd_norm_w, w_ssd_o, w_out, norm_ffn_w, w_router_group, b_router_group, w_router_expert,
           b_router_expert, w1, w3, w2):
    b, s, d = x.shape
    t = b * s
    depth = norm_mix_w.shape[0]
    cos, sin_signed = _rope_tables(s)
    x2d = x.reshape(t, d)
    sizes = (ATTN_W, KV_W, KV_W, SSD_W, CONV_CH, 2 * SSD_HEADS, 2 * d)
    offs = [0]
    for n in sizes:
        offs.append(offs[-1] + n)

    for l in range(depth):
        wi = w_in[l].astype(BF16)
        wq, wk, wv, wz, wxbc, wdt, wg = (wi[:, offs[j]:offs[j + 1]] for j in range(7))
        wdt = jnp.pad(wdt, ((0, 0), (0, LANES - 2 * SSD_HEADS)))
        q, k, v, z, xbc, dt_raw, g = _in_proj(x2d, norm_mix_w[l][None], wq, wk, wv, wz, wxbc, wdt, wg,
                                              b_gate[l][None])

        attn = _attention(q.reshape(b, s, ATTN_W), k, v, cos, sin_signed, q_norm_w[l][None], k_norm_w[l][None],
                          b, s)

        cw = jnp.pad(conv_w[l], ((0, SUBLANES - D_CONV), (0, 0)))
        ssd = _ssd(xbc.reshape(b, s, CONV_CH), dt_raw.reshape(b, s, LANES), z.reshape(b, s, SSD_W),
                   cw, conv_b[l][None], _pad_lanes(dt_bias[l].reshape(-1)), _pad_lanes(a_log[l].reshape(-1)),
                   jnp.repeat(d_skip[l], SSD_HEAD_DIM)[None], ssd_norm_w[l][None], b, s)

        wr = jnp.concatenate([w_router_expert[l], w_router_group[l]], axis=1)
        wr = jnp.pad(wr, ((0, 0), (0, LANES - wr.shape[1])))
        wr_hi = wr.astype(BF16)
        wr_lo = (wr - wr_hi.astype(F32)).astype(BF16)
        br = _pad_lanes(jnp.concatenate([b_router_expert[l], b_router_group[l]]))
        x2, h2p, route, counts = _out_proj(
            x2d, ssd.reshape(t, SSD_W), attn.reshape(t, ATTN_W), g,
            w_ssd_o[l].astype(BF16), w_attn_o[l].astype(BF16), w_out[l].astype(BF16),
            norm_ffn_w[l][None], jnp.concatenate([wr_hi, wr_lo], axis=1), br)

        x2d = _routed_moe(x2, h2p, route, counts, w1[l].astype(BF16), w3[l].astype(BF16), w2[l].astype(BF16))
    return x2d.reshape(b, s, d)
```

```python
import functools
import math

import jax
import jax.numpy as jnp
from jax import lax
from jax.experimental import pallas as pl
from jax.experimental.pallas import tpu as pltpu
from jax.experimental.pallas import tpu_sc as plsc

GRID_W = 64
HEAD_DIM = 64
N_Q_HEADS = 8
N_KV_HEADS = 2
GQA_GROUP = N_Q_HEADS // N_KV_HEADS
ATTN_W = N_Q_HEADS * HEAD_DIM
KV_W = N_KV_HEADS * HEAD_DIM
ROPE_THETA = 10000.0
SSD_HEAD_DIM = 64
SSD_HEADS = 8
SSD_W = SSD_HEADS * SSD_HEAD_DIM
SSD_GROUPS = 2
HEADS_PER_GROUP = SSD_HEADS // SSD_GROUPS
GROUP_W = HEADS_PER_GROUP * SSD_HEAD_DIM
D_STATE = 64
D_CONV = 7
CONV_PAD = (D_CONV - 1) // 2
CONV_CH = SSD_W + 2 * SSD_GROUPS * D_STATE
CHUNK = 128
N_EXPERT_GROUPS = 4
EXPERTS_PER_GROUP = 4
N_EXPERTS = N_EXPERT_GROUPS * EXPERTS_PER_GROUP
EPS = 1e-6

LANES = 128
SUBLANES = 8
VMEM_LIMIT = 56 * 1024 * 1024

SC_WINDOW = 128
SC_ROW_WORDS = 256

TM_PROJ = 256
TQ = 256
TM_MOE = 512

F32 = jnp.float32
BF16 = jnp.bfloat16
HIGHEST = lax.Precision.HIGHEST


def _dot(a, b):
    return jnp.dot(a, b, preferred_element_type=F32)


def _dot_nt(a, b):
    return lax.dot_general(a, b, (((1,), (1,)), ((), ())), preferred_element_type=F32)


def _dot_tn(a, b):
    return lax.dot_general(a, b, (((0,), (0,)), ((), ())), preferred_element_type=F32)


def _sigmoid(x):
    return 1.0 / (1.0 + jnp.exp(-x))


def _softplus(x):
    return jnp.maximum(x, 0.0) + jnp.log1p(jnp.exp(-jnp.abs(x)))


def _params(n_axes):
    return pltpu.CompilerParams(dimension_semantics=("arbitrary",) * n_axes,
                                vmem_limit_bytes=VMEM_LIMIT)


def _full(shape):
    return pl.BlockSpec(shape, lambda *_: (0,) * len(shape))


def _in_proj_kernel(x_ref, nw_ref, wq_ref, wk_ref, wv_ref, wz_ref, wxbc_ref, wdt_ref, wg_ref, bg_ref,
                    q_ref, k_ref, v_ref, z_ref, xbc_ref, dt_ref, g_ref):
    x = x_ref[...]
    ms = jnp.mean(x * x, axis=-1, keepdims=True)
    h = (x * lax.rsqrt(ms + EPS) * nw_ref[...]).astype(BF16)
    q_ref[...] = _dot(h, wq_ref[...])
    k = _dot(h, wk_ref[...])
    v = _dot(h, wv_ref[...]).astype(BF16)
    for j in range(N_KV_HEADS):
        k_ref[j] = k[:, j * HEAD_DIM:(j + 1) * HEAD_DIM]
        v_ref[j] = v[:, j * HEAD_DIM:(j + 1) * HEAD_DIM]
    z_ref[...] = _dot(h, wz_ref[...])
    xbc_ref[...] = _dot(h, wxbc_ref[...])
    dt_ref[...] = _dot(h, wdt_ref[...])
    g_ref[...] = _sigmoid(_dot(h, wg_ref[...]) + bg_ref[...]).astype(BF16)


def _in_proj(x2d, nw, wq, wk, wv, wz, wxbc, wdt, wg, bg):
    t, d = x2d.shape
    tm = TM_PROJ
    row = lambda n: pl.BlockSpec((tm, n), lambda i: (i, 0))
    return pl.pallas_call(
        _in_proj_kernel,
        grid=(t // tm,),
        in_specs=[row(d), _full(nw.shape), _full(wq.shape), _full(wk.shape), _full(wv.shape),
                  _full(wz.shape), _full(wxbc.shape), _full(wdt.shape), _full(wg.shape), _full(bg.shape)],
        out_specs=[row(ATTN_W),
                   pl.BlockSpec((N_KV_HEADS, tm, HEAD_DIM), lambda i: (0, i, 0)),
                   pl.BlockSpec((N_KV_HEADS, tm, HEAD_DIM), lambda i: (0, i, 0)),
                   row(SSD_W), row(CONV_CH), row(LANES), row(wg.shape[1])],
        out_shape=[jax.ShapeDtypeStruct((t, ATTN_W), F32),
                   jax.ShapeDtypeStruct((N_KV_HEADS, t, HEAD_DIM), F32),
                   jax.ShapeDtypeStruct((N_KV_HEADS, t, HEAD_DIM), BF16),
                   jax.ShapeDtypeStruct((t, SSD_W), F32),
                   jax.ShapeDtypeStruct((t, CONV_CH), F32),
                   jax.ShapeDtypeStruct((t, LANES), F32),
                   jax.ShapeDtypeStruct((t, wg.shape[1]), BF16)],
        compiler_params=_params(1),
        name="in_proj",
    )(x2d, nw, wq, wk, wv, wz, wxbc, wdt, wg, bg)


def _rope_partner(u):
    q = HEAD_DIM // 4
    return jnp.concatenate([u[:, q:2 * q], u[:, 0:q], u[:, 3 * q:4 * q], u[:, 2 * q:3 * q]], axis=-1)


def _norm_rope(u, w, cos, sin_signed):
    ms = jnp.mean(u * u, axis=-1, keepdims=True)
    un = u * lax.rsqrt(ms + EPS) * w
    return un * cos + _rope_partner(un) * sin_signed


def _attn_kernel(q_ref, k_ref, v_ref, cosq_ref, sinq_ref, cosk_ref, sink_ref, qw_ref, kw_ref,
                 o_ref, ks_ref, vbd_ref):
    gw = GQA_GROUP * HEAD_DIM

    @pl.when(pl.program_id(2) == 0)
    def _():
        ks_ref[...] = _norm_rope(k_ref[0], kw_ref[...], cosk_ref[...], sink_ref[...]).astype(BF16)
        v = v_ref[0]
        for g in range(GQA_GROUP):
            pieces = []
            if g > 0:
                pieces.append(jnp.zeros((v.shape[0], g * HEAD_DIM), BF16))
            pieces.append(v)
            if g < GQA_GROUP - 1:
                pieces.append(jnp.zeros((v.shape[0], gw - (g + 1) * HEAD_DIM), BF16))
            vbd_ref[g] = jnp.concatenate(pieces, axis=-1)

    q = q_ref[0]
    cos = cosq_ref[...]
    sin = sinq_ref[...]
    scale = math.log2(math.e) / math.sqrt(HEAD_DIM)
    acc = jnp.zeros((q.shape[0], gw), F32)
    inv_l = []
    for g in range(GQA_GROUP):
        qg = _norm_rope(q[:, g * HEAD_DIM:(g + 1) * HEAD_DIM], qw_ref[...], cos, sin) * scale
        s = _dot_nt(qg.astype(BF16), ks_ref[...])
        m = jnp.max(s, axis=-1, keepdims=True)
        p = jnp.exp2(s - m)
        l = jnp.sum(p, axis=-1, keepdims=True)
        acc = acc + _dot(p.astype(BF16), vbd_ref[g])
        inv_l.append(jnp.broadcast_to(1.0 / l, (q.shape[0], HEAD_DIM)))
    o_ref[0] = (acc * jnp.concatenate(inv_l, axis=-1)).astype(BF16)


def _attention(q, k, v, cos, sin_signed, qw, kw, b, s):
    tq = TQ
    gw = GQA_GROUP * HEAD_DIM
    return pl.pallas_call(
        _attn_kernel,
        grid=(b, N_KV_HEADS, s // tq),
        in_specs=[pl.BlockSpec((1, tq, gw), lambda bi, kv, qi: (bi, qi, kv)),
                  pl.BlockSpec((1, s, HEAD_DIM), lambda bi, kv, qi: (kv, bi, 0)),
                  pl.BlockSpec((1, s, HEAD_DIM), lambda bi, kv, qi: (kv, bi, 0)),
                  pl.BlockSpec((tq, HEAD_DIM), lambda bi, kv, qi: (qi, 0)),
                  pl.BlockSpec((tq, HEAD_DIM), lambda bi, kv, qi: (qi, 0)),
                  _full((s, HEAD_DIM)), _full((s, HEAD_DIM)),
                  _full((1, HEAD_DIM)), _full((1, HEAD_DIM))],
        out_specs=pl.BlockSpec((1, tq, gw), lambda bi, kv, qi: (bi, qi, kv)),
        out_shape=jax.ShapeDtypeStruct((b, s, ATTN_W), BF16),
        scratch_shapes=[pltpu.VMEM((s, HEAD_DIM), BF16),
                        pltpu.VMEM((GQA_GROUP, s, gw), BF16)],
        compiler_params=_params(3),
        name="attention",
    )(q, k, v, cos, sin_signed, cos, sin_signed, qw, kw)


def _conv_silu(cur_ref, prev_ref, next_ref, cw_ref, cb_ref, has_prev, has_next):
    prev = jnp.where(has_prev, prev_ref[0], 0.0)
    nxt = jnp.where(has_next, next_ref[0], 0.0)
    xin = jnp.concatenate([prev, cur_ref[0], nxt], axis=0)
    acc = jnp.zeros((CHUNK, CONV_CH), F32) + cb_ref[...]
    for t in range(D_CONV):
        off = SUBLANES - CONV_PAD + t
        acc = acc + xin[off:off + CHUNK, :] * cw_ref[t:t + 1, :]
    return acc * _sigmoid(acc)


def _scan_chunk(direction, xc, dtraw, bias, alog, tri_incl, tri_mask, state_ref):
    dt_all = _softplus(dtraw + bias)
    dta = dt_all * (-jnp.exp(alog))
    dta_t = jnp.transpose(dta)[0:2 * SSD_HEADS, :]
    u_r = jnp.dot(dta_t, tri_incl, precision=HIGHEST, preferred_element_type=F32)
    u_c = jnp.dot(tri_mask, dta, precision=HIGHEST, preferred_element_type=F32)
    tot = jnp.sum(dta, axis=0, keepdims=True)
    xs = xc[:, :SSD_W]
    ys = []
    for g in range(SSD_GROUPS):
        bg = xc[:, SSD_W + g * D_STATE:SSD_W + (g + 1) * D_STATE]
        cg = xc[:, SSD_W + (SSD_GROUPS + g) * D_STATE:SSD_W + (SSD_GROUPS + g + 1) * D_STATE]
        cg16 = cg.astype(BF16)
        cb = _dot_nt(cg16, bg.astype(BF16))
        st = state_ref[direction, g]
        y_off = _dot(cg16, st.astype(BF16))
        xw, keep = [], []
        for hh in range(HEADS_PER_GROUP):
            h = g * HEADS_PER_GROUP + hh
            col = direction * SSD_HEADS + h
            uc = u_c[:, col:col + 1]
            decay = jnp.exp(jnp.where(tri_mask > 0.0, uc - u_r[col:col + 1, :], -jnp.inf))
            xdt = xs[:, h * SSD_HEAD_DIM:(h + 1) * SSD_HEAD_DIM] * dt_all[:, col:col + 1]
            y_diag = _dot((cb * decay).astype(BF16), xdt.astype(BF16))
            ys.append(y_diag + y_off[:, hh * SSD_HEAD_DIM:(hh + 1) * SSD_HEAD_DIM] * jnp.exp(uc))
            tot_h = tot[:, col:col + 1]
            xw.append((xdt * jnp.exp(tot_h - uc)).astype(BF16))
            keep.append(jnp.broadcast_to(jnp.exp(tot_h), (1, SSD_HEAD_DIM)))
        state_ref[direction, g] = (st * jnp.concatenate(keep, axis=-1)
                                   + _dot_tn(bg.astype(BF16), jnp.concatenate(xw, axis=-1)))
    return jnp.concatenate(ys, axis=-1)


def _gate_norm(y, z, w):
    y = y * (z * _sigmoid(z))
    ms = jnp.mean(y * y, axis=-1, keepdims=True)
    return (y * lax.rsqrt(ms + EPS) * w).astype(BF16)


def _ssd_kernel(nc, curf_ref, prevf_ref, nextf_ref, curb_ref, prevb_ref, nextb_ref, dtf_ref, dtb_ref,
                zf_ref, zb_ref, cw_ref, cb_ref, bias_ref, alog_ref, mlow_ref, mup_ref, dskip_ref, nw_ref,
                o_ref, xc_ref, yacc_ref, state_ref):
    c = pl.program_id(1)
    cbk = nc - 1 - c
    first_visit = c < nc // 2
    rows_f = pl.ds(pl.multiple_of(c * CHUNK, CHUNK), CHUNK)
    rows_b = pl.ds(pl.multiple_of(cbk * CHUNK, CHUNK), CHUNK)

    @pl.when(c == 0)
    def _():
        state_ref[...] = jnp.zeros_like(state_ref)

    @pl.when(first_visit)
    def _():
        xc_ref[rows_f, :] = _conv_silu(curf_ref, prevf_ref, nextf_ref, cw_ref, cb_ref, c > 0, c < nc - 1)
        xc_ref[rows_b, :] = _conv_silu(curb_ref, prevb_ref, nextb_ref, cw_ref, cb_ref, cbk > 0, cbk < nc - 1)

    xcf = xc_ref[rows_f, :]
    xcb = xc_ref[rows_b, :]
    mlow = mlow_ref[...]
    mup = mup_ref[...]
    yf = _scan_chunk(0, xcf, dtf_ref[0], bias_ref[...], alog_ref[...], mup, mlow, state_ref)
    yb = _scan_chunk(1, xcb, dtb_ref[0], bias_ref[...], alog_ref[...], mlow, mup, state_ref)
    yf = yf + dskip_ref[...] * xcf[:, :SSD_W]

    @pl.when(first_visit)
    def _():
        yacc_ref[rows_f, :] = yf
        yacc_ref[rows_b, :] = yb

    @pl.when(jnp.logical_not(first_visit))
    def _():
        o_ref[0, rows_f, :] = _gate_norm(yacc_ref[rows_f, :] + yf, zf_ref[0], nw_ref[...])
        o_ref[0, rows_b, :] = _gate_norm(yacc_ref[rows_b, :] + yb, zb_ref[0], nw_ref[...])


def _ssd(xbc, dt_raw, z, conv_w, conv_b, bias, alog, dskip, nw, b, s):
    nc = s // CHUNK
    assert nc % 2 == 0
    hb = CHUNK // SUBLANES
    nhb = s // SUBLANES
    half = nc // 2
    tri = jnp.tril(jnp.ones((CHUNK, CHUNK), F32))
    fpos = lambda c: jnp.minimum(c, half - 1)
    bpos = lambda c: jnp.maximum(nc - 1 - c, half)
    cur = lambda pos: pl.BlockSpec((1, CHUNK, CONV_CH), lambda bi, c: (bi, pos(c), 0))
    prev = lambda pos: pl.BlockSpec((1, SUBLANES, CONV_CH),
                                    lambda bi, c: (bi, jnp.maximum(pos(c) * hb - 1, 0), 0))
    nxt = lambda pos: pl.BlockSpec((1, SUBLANES, CONV_CH),
                                   lambda bi, c: (bi, jnp.minimum((pos(c) + 1) * hb, nhb - 1), 0))
    chunk_f = lambda n: pl.BlockSpec((1, CHUNK, n), lambda bi, c: (bi, c, 0))
    chunk_b = lambda n: pl.BlockSpec((1, CHUNK, n), lambda bi, c: (bi, nc - 1 - c, 0))
    return pl.pallas_call(
        functools.partial(_ssd_kernel, nc),
        grid=(b, nc),
        in_specs=[cur(fpos), prev(fpos), nxt(fpos), cur(bpos), prev(bpos), nxt(bpos),
                  chunk_f(LANES), chunk_b(LANES), chunk_f(SSD_W), chunk_b(SSD_W),
                  _full(conv_w.shape), _full(conv_b.shape), _full(bias.shape), _full(alog.shape),
                  _full(tri.shape), _full(tri.shape), _full(dskip.shape), _full(nw.shape)],
        out_specs=pl.BlockSpec((1, s, SSD_W), lambda bi, c: (bi, 0, 0)),
        out_shape=jax.ShapeDtypeStruct((b, s, SSD_W), BF16),
        scratch_shapes=[pltpu.VMEM((s, CONV_CH), F32),
                        pltpu.VMEM((s, SSD_W), F32),
                        pltpu.VMEM((2, SSD_GROUPS, D_STATE, GROUP_W), F32)],
        compiler_params=_params(2),
        name="ssd",
    )(xbc, xbc, xbc, xbc, xbc, xbc, dt_raw, dt_raw, z, z, conv_w, conv_b, bias, alog, tri, tri.T, dskip, nw)


def _pack_bf16_pairs(v):
    n = v.shape[1] // 2
    hi = pltpu.bitcast(v[:, :n].astype(BF16).astype(F32), jnp.uint32)
    lo = pltpu.bitcast(v[:, n:].astype(BF16).astype(F32), jnp.uint32)
    return hi | (lo >> 16)


def _unpack_bf16_pairs(u):
    hi = pltpu.bitcast(u & jnp.uint32(0xFFFF0000), F32)
    lo = pltpu.bitcast(u << 16, F32)
    return hi, lo


def _store_packed(v, refs):
    words = _pack_bf16_pairs(v)
    w = refs[0].shape[-1]
    for j, ref in enumerate(refs):
        ref[...] = words[:, j * w:(j + 1) * w]


def _load_packed(refs):
    parts = [_unpack_bf16_pairs(ref[...]) for ref in refs]
    return jnp.concatenate([p[0] for p in parts] + [p[1] for p in parts], axis=-1)


R_E1, R_E2, R_C1, R_C2, R_RANK1, R_RANK2 = range(6)


def _out_proj_kernel(x_ref, ssd_ref, attn_ref, g_ref, wso_ref, wao_ref, wout_ref,
                     fnw_ref, wr_ref, br_ref, tril_ref, x2_ref, h2a_ref, h2b_ref, route_ref, cnt_ref,
                     base_ref):
    @pl.when(pl.program_id(0) == 0)
    def _():
        base_ref[...] = jnp.zeros_like(base_ref)

    g = g_ref[...]
    dm = x_ref.shape[1]
    merged = g[:, :dm] * _dot(attn_ref[...], wao_ref[...]) + g[:, dm:] * _dot(ssd_ref[...], wso_ref[...])
    x2 = x_ref[...] + _dot(merged.astype(BF16), wout_ref[...])
    x2_ref[...] = x2

    ms2 = jnp.mean(x2 * x2, axis=-1, keepdims=True)
    h2 = x2 * lax.rsqrt(ms2 + EPS) * fnw_ref[...]
    h2_hi = h2.astype(BF16)
    _store_packed(h2, (h2a_ref, h2b_ref))

    h2_lo = (h2 - h2_hi.astype(F32)).astype(BF16)
    pp = _dot(h2_hi, wr_ref[...]) + _dot(h2_lo, wr_ref[...])
    lg = pp[:, :LANES] + pp[:, LANES:] + br_ref[...]
    lane = lax.broadcasted_iota(jnp.int32, lg.shape, 1)
    neg = -jnp.inf
    big = jnp.int32(LANES)
    is_g = (lane >= N_EXPERTS) & (lane < N_EXPERTS + N_EXPERT_GROUPS)
    gl = jnp.where(is_g, lg, neg)
    ge = jnp.exp(gl - jnp.max(gl, axis=-1, keepdims=True))
    pg = ge / jnp.sum(ge, axis=-1, keepdims=True)
    g_val = jnp.max(pg, axis=-1, keepdims=True)
    g_idx = jnp.min(jnp.where(is_g & (pg == g_val), lane, big), axis=-1, keepdims=True) - N_EXPERTS
    lo = g_idx * EXPERTS_PER_GROUP
    sel = (lane >= lo) & (lane < lo + EXPERTS_PER_GROUP)
    fl = jnp.where(sel, lg, neg)
    fe = jnp.exp(fl - jnp.max(fl, axis=-1, keepdims=True))
    pf = fe / jnp.sum(fe, axis=-1, keepdims=True)
    v1 = jnp.max(pf, axis=-1, keepdims=True)
    i1 = jnp.min(jnp.where(sel & (pf == v1), lane, big), axis=-1, keepdims=True)
    pf2 = jnp.where(sel & (lane != i1), pf, -1.0)
    v2 = jnp.max(pf2, axis=-1, keepdims=True)
    i2 = jnp.min(jnp.where(pf2 == v2, lane, big), axis=-1, keepdims=True)
    den = v1 + v2
    c1 = g_val * (v1 / den)
    c2 = g_val * (v2 / den)

    oh = jnp.where(lane == i1, 1.0, 0.0) + jnp.where(lane == i2, 1.0, 0.0)
    cnt = _dot(tril_ref[...], oh.astype(BF16)) + base_ref[...]
    r1 = jnp.sum(jnp.where(lane == i1, cnt, 0.0), axis=-1, keepdims=True)
    r2 = jnp.sum(jnp.where(lane == i2, cnt, 0.0), axis=-1, keepdims=True)
    base = base_ref[...] + jnp.sum(oh, axis=0, keepdims=True)
    base_ref[...] = base
    cnt_ref[...] = jnp.broadcast_to(base, cnt_ref.shape)

    rec = jnp.zeros(lg.shape, F32)
    for slot, val in ((R_E1, i1.astype(F32)), (R_E2, i2.astype(F32)), (R_C1, c1), (R_C2, c2),
                      (R_RANK1, r1), (R_RANK2, r2)):
        rec = jnp.where(lane == slot, val, rec)
    route_ref[...] = rec


def _out_proj(x2d, ssd, attn, g, wso, wao, wout, fnw, wr, br):
    t, d = x2d.shape
    tm = TM_PROJ
    row = lambda n: pl.BlockSpec((tm, n), lambda i: (i, 0))
    tril_strict = jnp.tril(jnp.ones((tm, tm), BF16), k=-1)
    return pl.pallas_call(
        _out_proj_kernel,
        grid=(t // tm,),
        in_specs=[row(d), row(SSD_W), row(ATTN_W), row(2 * d),
                  _full(wso.shape), _full(wao.shape), _full(wout.shape),
                  _full(fnw.shape), _full(wr.shape), _full(br.shape), _full(tril_strict.shape)],
        out_specs=[row(d), row(SC_ROW_WORDS), row(SC_ROW_WORDS), row(LANES), _full((SUBLANES, LANES))],
        out_shape=[jax.ShapeDtypeStruct((t, d), F32),
                   jax.ShapeDtypeStruct((t, SC_ROW_WORDS), jnp.uint32),
                   jax.ShapeDtypeStruct((t, SC_ROW_WORDS), jnp.uint32),
                   jax.ShapeDtypeStruct((t, LANES), F32),
                   jax.ShapeDtypeStruct((SUBLANES, LANES), F32)],
        scratch_shapes=[pltpu.VMEM((1, LANES), F32)],
        compiler_params=_params(1),
        name="out_proj",
    )(x2d, ssd, attn, g, wso, wao, wout, fnw, wr, br, tril_strict)


def _sc_mesh():
    return plsc.VectorSubcoreMesh(core_axis_name="core", subcore_axis_name="subcore")


def _sc_scatter_rows(rows, idx_a, idx_b, n_out):
    n, w = rows.shape

    @pl.kernel(out_type=jax.ShapeDtypeStruct((n_out, w), rows.dtype), mesh=_sc_mesh(), name="moe_scatter")
    def scatter(x_hbm, ia_hbm, ib_hbm, o_hbm):
        def body(x_vmem, ia_vmem, ib_vmem):
            pltpu.sync_copy(x_vmem, o_hbm.at[ia_vmem.at[0]])
            pltpu.sync_copy(x_vmem, o_hbm.at[ib_vmem.at[0]])

        pltpu.emit_pipeline(
            body, grid=(n // SC_WINDOW,),
            in_specs=[pl.BlockSpec((SC_WINDOW, w), lambda i: (i, 0)),
                      pl.BlockSpec((1, SC_WINDOW), lambda i: (0, i)),
                      pl.BlockSpec((1, SC_WINDOW), lambda i: (0, i))],
            out_specs=[],
            core_axis_name=("core", "subcore"),
            dimension_semantics=(pltpu.PARALLEL,),
        )(x_hbm, ia_hbm, ib_hbm)

    return scatter(rows, idx_a.reshape(1, n), idx_b.reshape(1, n))


def _sc_gather_rows(table, idx):
    m = idx.shape[0]
    w = table.shape[1]

    @pl.kernel(out_type=jax.ShapeDtypeStruct((m, w), table.dtype), mesh=_sc_mesh(), name="moe_gather")
    def gather(x_hbm, i_hbm, o_hbm):
        def body(i_vmem, o_vmem):
            pltpu.sync_copy(x_hbm.at[i_vmem.at[0]], o_vmem)

        pltpu.emit_pipeline(
            body, grid=(m // SC_WINDOW,),
            in_specs=[pl.BlockSpec((1, SC_WINDOW), lambda i: (0, i))],
            out_specs=[pl.BlockSpec((SC_WINDOW, w), lambda i: (i, 0))],
            core_axis_name=("core", "subcore"),
            dimension_semantics=(pltpu.PARALLEL,),
        )(i_hbm, o_hbm)

    return gather(table, idx.reshape(1, m))


def _gmm_kernel(te_ref, nt_ref, xa_ref, xb_ref, w1_ref, w3_ref, w2_ref, ya_ref, yb_ref):
    @pl.when(pl.program_id(0) < nt_ref[0])
    def _():
        x = _load_packed((xa_ref, xb_ref)).astype(BF16)
        a = _dot(x, w1_ref[0])
        act = ((a * _sigmoid(a)) * _dot(x, w3_ref[0])).astype(BF16)
        _store_packed(_dot(act, w2_ref[0]), (ya_ref, yb_ref))


def _gmm(tile_expert, n_tiles, xs_a, xs_b, w1, w3, w2):
    p, words = xs_a.shape
    tm = TM_MOE
    d, de = w1.shape[1], w1.shape[2]
    row = pl.BlockSpec((tm, words), lambda i, te, nt: (jnp.minimum(i, nt[0] - 1), 0))
    packed = jax.ShapeDtypeStruct((p, words), jnp.uint32)
    return pl.pallas_call(
        _gmm_kernel,
        grid_spec=pltpu.PrefetchScalarGridSpec(
            num_scalar_prefetch=2,
            grid=(p // tm,),
            in_specs=[row, row,
                      pl.BlockSpec((1, d, de), lambda i, te, nt: (te[i], 0, 0)),
                      pl.BlockSpec((1, d, de), lambda i, te, nt: (te[i], 0, 0)),
                      pl.BlockSpec((1, de, d), lambda i, te, nt: (te[i], 0, 0))],
            out_specs=[row, row]),
        out_shape=[packed, packed],
        compiler_params=_params(1),
        name="moe_gmm",
    )(tile_expert, n_tiles, xs_a, xs_b, w1, w3, w2)


def _combine_kernel(x2_ref, y1a_ref, y1b_ref, y2a_ref, y2b_ref, route_ref, o_ref):
    route = route_ref[...]
    c1 = route[:, R_C1:R_C1 + 1]
    c2 = route[:, R_C2:R_C2 + 1]
    o_ref[...] = x2_ref[...] + (c1 * _load_packed((y1a_ref, y1b_ref)) + c2 * _load_packed((y2a_ref, y2b_ref)))


def _combine(x2, y12_a, y12_b, route):
    t, d = x2.shape
    tm = TM_MOE
    nb = t // tm
    words = y12_a.shape[1]
    first = pl.BlockSpec((tm, words), lambda i: (i, 0))
    second = pl.BlockSpec((tm, words), lambda i: (i + nb, 0))
    return pl.pallas_call(
        _combine_kernel,
        grid=(nb,),
        in_specs=[pl.BlockSpec((tm, d), lambda i: (i, 0)), first, first, second, second,
                  pl.BlockSpec((tm, LANES), lambda i: (i, 0))],
        out_specs=pl.BlockSpec((tm, d), lambda i: (i, 0)),
        out_shape=jax.ShapeDtypeStruct((t, d), F32),
        compiler_params=_params(1),
        name="moe_combine",
    )(x2, y12_a, y12_b, y12_a, y12_b, route)


def _routed_moe(x2, h2_a, h2_b, route, counts, w1, w3, w2):
    t = x2.shape[0]
    tm = TM_MOE
    p_max = 2 * t + N_EXPERTS * tm
    experts = jnp.arange(N_EXPERTS, dtype=jnp.int32)
    e12 = route[:, R_E1:R_E2 + 1].astype(jnp.int32)
    r12 = route[:, R_RANK1:R_RANK2 + 1].astype(jnp.int32)
    cnt = counts[0, :N_EXPERTS].astype(jnp.int32)
    padded = (cnt + tm - 1) // tm * tm
    ends = jnp.cumsum(padded)
    starts = ends - padded
    pos = jnp.sum(jnp.where(e12[..., None] == experts, starts, 0), axis=-1) + r12
    tile_ends = ends // tm
    tiles = jnp.arange(p_max // tm, dtype=jnp.int32)
    tile_expert = jnp.minimum(jnp.sum((tiles[:, None] >= tile_ends[None, :]).astype(jnp.int32), axis=1),
                              N_EXPERTS - 1)
    n_tiles = tile_ends[-1:].astype(jnp.int32)

    pos1, pos2 = pos[:, 0], pos[:, 1]
    xs_a = _sc_scatter_rows(h2_a, pos1, pos2, p_max)
    xs_b = _sc_scatter_rows(h2_b, pos1, pos2, p_max)
    y_a, y_b = _gmm(tile_expert, n_tiles, xs_a, xs_b, w1, w3, w2)
    pos12 = jnp.concatenate([pos1, pos2])
    return _combine(x2, _sc_gather_rows(y_a, pos12), _sc_gather_rows(y_b, pos12), route)


def _rope_tables(s):
    rows = s // GRID_W
    row = jnp.repeat(jnp.arange(rows, dtype=jnp.int32), GRID_W)
    col = jnp.tile(jnp.arange(GRID_W, dtype=jnp.int32), rows)
    half = HEAD_DIM // 2
    inv_freq = ROPE_THETA ** (-jnp.arange(0, half, 2, dtype=F32) / half)
    ang_r = row.astype(F32)[:, None] * inv_freq[None, :]
    ang_c = col.astype(F32)[:, None] * inv_freq[None, :]
    cos = jnp.concatenate([jnp.cos(ang_r), jnp.cos(ang_r), jnp.cos(ang_c), jnp.cos(ang_c)], axis=-1)
    sin_signed = jnp.concatenate([-jnp.sin(ang_r), jnp.sin(ang_r), -jnp.sin(ang_c), jnp.sin(ang_c)], axis=-1)
    return cos, sin_signed


def _pad_lanes(v):
    return jnp.pad(v, (0, LANES - v.shape[0]))[None]


def kernel(x, norm_mix_w, w_in, b_gate, q_norm_w, k_norm_w, w_attn_o, conv_w, conv_b, dt_bias, a_log, d_skip,
           ssd_norm_w, w_ssd_o, w_out, norm_ffn_w, w_router_group, b_router_group, w_router_expert,
           b_router_expert, w1, w3, w2):
    b, s, d = x.shape
    t = b * s
    depth = norm_mix_w.shape[0]
    cos, sin_signed = _rope_tables(s)
    x2d = x.reshape(t, d)
    sizes = (ATTN_W, KV_W, KV_W, SSD_W, CONV_CH, 2 * SSD_HEADS, 2 * d)
    offs = [0]
    for n in sizes:
        offs.append(offs[-1] + n)

    for l in range(depth):
        wi = w_in[l].astype(BF16)
        wq, wk, wv, wz, wxbc, wdt, wg = (wi[:, offs[j]:offs[j + 1]] for j in range(7))
        wdt = jnp.pad(wdt, ((0, 0), (0, LANES - 2 * SSD_HEADS)))
        q, k, v, z, xbc, dt_raw, g = _in_proj(x2d, norm_mix_w[l][None], wq, wk, wv, wz, wxbc, wdt, wg,
                                              b_gate[l][None])

        attn = _attention(q.reshape(b, s, ATTN_W), k, v, cos, sin_signed, q_norm_w[l][None], k_norm_w[l][None],
                          b, s)

        cw = jnp.pad(conv_w[l], ((0, SUBLANES - D_CONV), (0, 0)))
        ssd = _ssd(xbc.reshape(b, s, CONV_CH), dt_raw.reshape(b, s, LANES), z.reshape(b, s, SSD_W),
                   cw, conv_b[l][None], _pad_lanes(dt_bias[l].reshape(-1)), _pad_lanes(a_log[l].reshape(-1)),
                   jnp.repeat(d_skip[l], SSD_HEAD_DIM)[None], ssd_norm_w[l][None], b, s)

        wr = jnp.concatenate([w_router_expert[l], w_router_group[l]], axis=1)
        wr = jnp.pad(wr, ((0, 0), (0, LANES - wr.shape[1])))
        wr_hi = wr.astype(BF16)
        wr_lo = (wr - wr_hi.astype(F32)).astype(BF16)
        br = _pad_lanes(jnp.concatenate([b_router_expert[l], b_router_group[l]]))
        x2, h2_a, h2_b, route, counts = _out_proj(
            x2d, ssd.reshape(t, SSD_W), attn.reshape(t, ATTN_W), g,
            w_ssd_o[l].astype(BF16), w_attn_o[l].astype(BF16), w_out[l].astype(BF16),
            norm_ffn_w[l][None], jnp.concatenate([wr_hi, wr_lo], axis=1), br)

        x2d = _routed_moe(x2, h2_a, h2_b, route, counts,
                          w1[l].astype(BF16), w3[l].astype(BF16), w2[l].astype(BF16))
    return x2d.reshape(b, s, d)
```

```python
import functools
import math

import jax
import jax.numpy as jnp
from jax import lax
from jax.experimental import pallas as pl
from jax.experimental.pallas import tpu as pltpu
from jax.experimental.pallas import tpu_sc as plsc

GRID_W = 64
HEAD_DIM = 64
N_Q_HEADS = 8
N_KV_HEADS = 2
GQA_GROUP = N_Q_HEADS // N_KV_HEADS
ATTN_W = N_Q_HEADS * HEAD_DIM
KV_W = N_KV_HEADS * HEAD_DIM
ROPE_THETA = 10000.0
SSD_HEAD_DIM = 64
SSD_HEADS = 8
SSD_W = SSD_HEADS * SSD_HEAD_DIM
SSD_GROUPS = 2
HEADS_PER_GROUP = SSD_HEADS // SSD_GROUPS
GROUP_W = HEADS_PER_GROUP * SSD_HEAD_DIM
D_STATE = 64
D_CONV = 7
CONV_PAD = (D_CONV - 1) // 2
CONV_CH = SSD_W + 2 * SSD_GROUPS * D_STATE
CHUNK = 128
N_EXPERT_GROUPS = 4
EXPERTS_PER_GROUP = 4
N_EXPERTS = N_EXPERT_GROUPS * EXPERTS_PER_GROUP
EPS = 1e-6

LANES = 128
SUBLANES = 8
VMEM_LIMIT = 56 * 1024 * 1024

SC_WINDOW = 128
SC_ROW_WORDS = 256

TM_PROJ = 256
TQ = 256
TK = 2048
TM_MOE = 512

F32 = jnp.float32
BF16 = jnp.bfloat16
HIGHEST = lax.Precision.HIGHEST


def _dot(a, b):
    return jnp.dot(a, b, preferred_element_type=F32)


def _dot_nt(a, b):
    return lax.dot_general(a, b, (((1,), (1,)), ((), ())), preferred_element_type=F32)


def _dot_tn(a, b):
    return lax.dot_general(a, b, (((0,), (0,)), ((), ())), preferred_element_type=F32)


def _sigmoid(x):
    return 1.0 / (1.0 + jnp.exp(-x))


def _softplus(x):
    return jnp.maximum(x, 0.0) + jnp.log1p(jnp.exp(-jnp.abs(x)))


def _params(n_axes):
    return pltpu.CompilerParams(dimension_semantics=("arbitrary",) * n_axes,
                                vmem_limit_bytes=VMEM_LIMIT)


def _full(shape):
    return pl.BlockSpec(shape, lambda *_: (0,) * len(shape))


def _in_proj_kernel(x_ref, nw_ref, wq_ref, wk_ref, wv_ref, wz_ref, wxbc_ref, wdt_ref, wg_ref, bg_ref,
                    q_ref, k_ref, v_ref, z_ref, xbc_ref, dt_ref, g_ref):
    x = x_ref[...]
    ms = jnp.mean(x * x, axis=-1, keepdims=True)
    h = (x * lax.rsqrt(ms + EPS) * nw_ref[...]).astype(BF16)
    q_ref[...] = _dot(h, wq_ref[...])
    k = _dot(h, wk_ref[...])
    v = _dot(h, wv_ref[...]).astype(BF16)
    for j in range(N_KV_HEADS):
        k_ref[j] = k[:, j * HEAD_DIM:(j + 1) * HEAD_DIM]
        v_ref[j] = v[:, j * HEAD_DIM:(j + 1) * HEAD_DIM]
    z_ref[...] = _dot(h, wz_ref[...])
    xbc_ref[...] = _dot(h, wxbc_ref[...])
    dt_ref[...] = _dot(h, wdt_ref[...])
    g_ref[...] = _sigmoid(_dot(h, wg_ref[...]) + bg_ref[...]).astype(BF16)


def _in_proj(x2d, nw, wq, wk, wv, wz, wxbc, wdt, wg, bg):
    t, d = x2d.shape
    tm = TM_PROJ
    row = lambda n: pl.BlockSpec((tm, n), lambda i: (i, 0))
    return pl.pallas_call(
        _in_proj_kernel,
        grid=(t // tm,),
        in_specs=[row(d), _full(nw.shape), _full(wq.shape), _full(wk.shape), _full(wv.shape),
                  _full(wz.shape), _full(wxbc.shape), _full(wdt.shape), _full(wg.shape), _full(bg.shape)],
        out_specs=[row(ATTN_W),
                   pl.BlockSpec((N_KV_HEADS, tm, HEAD_DIM), lambda i: (0, i, 0)),
                   pl.BlockSpec((N_KV_HEADS, tm, HEAD_DIM), lambda i: (0, i, 0)),
                   row(SSD_W), row(CONV_CH), row(LANES), row(wg.shape[1])],
        out_shape=[jax.ShapeDtypeStruct((t, ATTN_W), F32),
                   jax.ShapeDtypeStruct((N_KV_HEADS, t, HEAD_DIM), F32),
                   jax.ShapeDtypeStruct((N_KV_HEADS, t, HEAD_DIM), BF16),
                   jax.ShapeDtypeStruct((t, SSD_W), F32),
                   jax.ShapeDtypeStruct((t, CONV_CH), F32),
                   jax.ShapeDtypeStruct((t, LANES), F32),
                   jax.ShapeDtypeStruct((t, wg.shape[1]), BF16)],
        compiler_params=_params(1),
        name="in_proj",
    )(x2d, nw, wq, wk, wv, wz, wxbc, wdt, wg, bg)


def _rope_partner(u):
    q = HEAD_DIM // 4
    return jnp.concatenate([u[:, q:2 * q], u[:, 0:q], u[:, 3 * q:4 * q], u[:, 2 * q:3 * q]], axis=-1)


def _norm_rope(u, w, cos, sin_signed):
    ms = jnp.mean(u * u, axis=-1, keepdims=True)
    un = u * lax.rsqrt(ms + EPS) * w
    return un * cos + _rope_partner(un) * sin_signed


def _attn_kernel(q_ref, k_ref, v_ref, cosq_ref, sinq_ref, cosk_ref, sink_ref, qw_ref, kw_ref,
                 o_ref, ks_ref):
    @pl.when(pl.program_id(2) == 0)
    def _():
        ks_ref[...] = _norm_rope(k_ref[0], kw_ref[...], cosk_ref[...], sink_ref[...]).astype(BF16)

    q = q_ref[0]
    tq = q.shape[0]
    cos = cosq_ref[...]
    sin = sinq_ref[...]
    scale = math.log2(math.e) / math.sqrt(HEAD_DIM)
    qs = jnp.concatenate(
        [(_norm_rope(q[:, g * HEAD_DIM:(g + 1) * HEAD_DIM], qw_ref[...], cos, sin) * scale).astype(BF16)
         for g in range(GQA_GROUP)], axis=0)
    rows = GQA_GROUP * tq
    m = jnp.full((rows, 1), -jnp.inf, F32)
    l = jnp.zeros((rows, 1), F32)
    acc = jnp.zeros((rows, HEAD_DIM), F32)
    tk = min(TK, ks_ref.shape[0])
    for kb in range(ks_ref.shape[0] // tk):
        keys = slice(kb * tk, (kb + 1) * tk)
        s = _dot_nt(qs, ks_ref[keys, :])
        m_new = jnp.maximum(m, jnp.max(s, axis=-1, keepdims=True))
        alpha = jnp.exp2(m - m_new)
        p = jnp.exp2(s - m_new)
        l = l * alpha + jnp.sum(p, axis=-1, keepdims=True)
        acc = acc * alpha + _dot(p.astype(BF16), v_ref[0, keys, :])
        m = m_new
    o = acc / l
    o_ref[0] = jnp.concatenate([o[g * tq:(g + 1) * tq] for g in range(GQA_GROUP)], axis=-1).astype(BF16)


def _attention(q, k, v, cos, sin_signed, qw, kw, b, s):
    tq = TQ
    gw = GQA_GROUP * HEAD_DIM
    return pl.pallas_call(
        _attn_kernel,
        grid=(b, N_KV_HEADS, s // tq),
        in_specs=[pl.BlockSpec((1, tq, gw), lambda bi, kv, qi: (bi, qi, kv)),
                  pl.BlockSpec((1, s, HEAD_DIM), lambda bi, kv, qi: (kv, bi, 0)),
                  pl.BlockSpec((1, s, HEAD_DIM), lambda bi, kv, qi: (kv, bi, 0)),
                  pl.BlockSpec((tq, HEAD_DIM), lambda bi, kv, qi: (qi, 0)),
                  pl.BlockSpec((tq, HEAD_DIM), lambda bi, kv, qi: (qi, 0)),
                  _full((s, HEAD_DIM)), _full((s, HEAD_DIM)),
                  _full((1, HEAD_DIM)), _full((1, HEAD_DIM))],
        out_specs=pl.BlockSpec((1, tq, gw), lambda bi, kv, qi: (bi, qi, kv)),
        out_shape=jax.ShapeDtypeStruct((b, s, ATTN_W), BF16),
        scratch_shapes=[pltpu.VMEM((s, HEAD_DIM), BF16)],
        compiler_params=_params(3),
        name="attention",
    )(q, k, v, cos, sin_signed, cos, sin_signed, qw, kw)


def _conv_silu(cur_ref, prev_ref, next_ref, cw_ref, cb_ref, has_prev, has_next):
    prev = jnp.where(has_prev, prev_ref[0], 0.0)
    nxt = jnp.where(has_next, next_ref[0], 0.0)
    xin = jnp.concatenate([prev, cur_ref[0], nxt], axis=0)
    acc = jnp.zeros((CHUNK, CONV_CH), F32) + cb_ref[...]
    for t in range(D_CONV):
        off = SUBLANES - CONV_PAD + t
        acc = acc + xin[off:off + CHUNK, :] * cw_ref[t:t + 1, :]
    return acc * _sigmoid(acc)


def _split3(a):
    a1 = a.astype(BF16)
    r1 = a - a1.astype(F32)
    a2 = r1.astype(BF16)
    a3 = (r1 - a2.astype(F32)).astype(BF16)
    return a1, a2, a3


def _scan_chunk(direction, xc, dtraw, bias, alog, alog_x, tri_incl, tri_mask, expand, state_ref):
    dt_all = _softplus(dtraw + bias)
    h0 = direction * SSD_HEADS
    dta_t = jnp.transpose(dt_all * (-jnp.exp(alog)))[h0:h0 + SSD_HEADS, :]
    u_r = jnp.dot(dta_t, tri_incl, precision=HIGHEST, preferred_element_type=F32)

    dt_x = sum(_dot(piece, expand) for piece in _split3(dt_all))
    dta_x = dt_x * (-jnp.exp(alog_x))
    tri16 = tri_mask.astype(BF16)
    u_x = sum(_dot(tri16, piece) for piece in _split3(dta_x))
    tot_x = jnp.sum(dta_x, axis=0, keepdims=True)

    xdt = xc[:, :SSD_W] * dt_x
    xdt16 = xdt.astype(BF16)
    xw16 = (xdt * jnp.exp(tot_x - u_x)).astype(BF16)
    e_u = jnp.exp(u_x)
    keep = jnp.exp(tot_x)
    ys = []
    for g in range(SSD_GROUPS):
        gcols = slice(g * GROUP_W, (g + 1) * GROUP_W)
        bg16 = xc[:, SSD_W + g * D_STATE:SSD_W + (g + 1) * D_STATE].astype(BF16)
        cg16 = xc[:, SSD_W + (SSD_GROUPS + g) * D_STATE:SSD_W + (SSD_GROUPS + g + 1) * D_STATE].astype(BF16)
        cb = _dot_nt(cg16, bg16)
        st = state_ref[direction, g]
        y_diag = []
        for hh in range(HEADS_PER_GROUP):
            h = g * HEADS_PER_GROUP + hh
            hcols = slice(h * SSD_HEAD_DIM, (h + 1) * SSD_HEAD_DIM)
            uc = u_x[:, h * SSD_HEAD_DIM:h * SSD_HEAD_DIM + 1]
            decay = jnp.exp(jnp.where(tri_mask > 0.0, uc - u_r[h:h + 1, :], -jnp.inf))
            y_diag.append(_dot((cb * decay).astype(BF16), xdt16[:, hcols]))
        ys.append(jnp.concatenate(y_diag, axis=-1) + _dot(cg16, st.astype(BF16)) * e_u[:, gcols])
        state_ref[direction, g] = st * keep[:, gcols] + _dot_tn(bg16, xw16[:, gcols])
    return jnp.concatenate(ys, axis=-1)


def _gate_norm(y, z, w):
    y = y * (z * _sigmoid(z))
    ms = jnp.mean(y * y, axis=-1, keepdims=True)
    return (y * lax.rsqrt(ms + EPS) * w).astype(BF16)


def _ssd_kernel(nc, curf_ref, prevf_ref, nextf_ref, curb_ref, prevb_ref, nextb_ref, dtf_ref, dtb_ref,
                zf_ref, zb_ref, cw_ref, cb_ref, bias_ref, alog_ref, alogx_ref, mlow_ref, mup_ref, exp_ref,
                dskip_ref, nw_ref, o_ref, xc_ref, yacc_ref, state_ref):
    c = pl.program_id(1)
    cbk = nc - 1 - c
    first_visit = c < nc // 2
    rows_f = pl.ds(pl.multiple_of(c * CHUNK, CHUNK), CHUNK)
    rows_b = pl.ds(pl.multiple_of(cbk * CHUNK, CHUNK), CHUNK)

    @pl.when(c == 0)
    def _():
        state_ref[...] = jnp.zeros_like(state_ref)

    @pl.when(first_visit)
    def _():
        xc_ref[rows_f, :] = _conv_silu(curf_ref, prevf_ref, nextf_ref, cw_ref, cb_ref, c > 0, c < nc - 1)
        xc_ref[rows_b, :] = _conv_silu(curb_ref, prevb_ref, nextb_ref, cw_ref, cb_ref, cbk > 0, cbk < nc - 1)

    xcf = xc_ref[rows_f, :]
    xcb = xc_ref[rows_b, :]
    mlow = mlow_ref[...]
    mup = mup_ref[...]
    yf = _scan_chunk(0, xcf, dtf_ref[0], bias_ref[...], alog_ref[...], alogx_ref[0], mup, mlow, exp_ref[0],
                     state_ref)
    yb = _scan_chunk(1, xcb, dtb_ref[0], bias_ref[...], alog_ref[...], alogx_ref[1], mlow, mup, exp_ref[1],
                     state_ref)
    yf = yf + dskip_ref[...] * xcf[:, :SSD_W]

    @pl.when(first_visit)
    def _():
        yacc_ref[rows_f, :] = yf
        yacc_ref[rows_b, :] = yb

    @pl.when(jnp.logical_not(first_visit))
    def _():
        o_ref[0, rows_f, :] = _gate_norm(yacc_ref[rows_f, :] + yf, zf_ref[0], nw_ref[...])
        o_ref[0, rows_b, :] = _gate_norm(yacc_ref[rows_b, :] + yb, zb_ref[0], nw_ref[...])


def _ssd(xbc, dt_raw, z, conv_w, conv_b, bias, alog, alog_x, dskip, nw, b, s):
    nc = s // CHUNK
    assert nc % 2 == 0
    hb = CHUNK // SUBLANES
    nhb = s // SUBLANES
    half = nc // 2
    tri = jnp.tril(jnp.ones((CHUNK, CHUNK), F32))
    lane_head = jnp.arange(LANES)[None, :, None] - SSD_HEADS * jnp.arange(2)[:, None, None]
    expand = (lane_head == (jnp.arange(SSD_W) // SSD_HEAD_DIM)[None, None, :]).astype(BF16)
    fpos = lambda c: jnp.minimum(c, half - 1)
    bpos = lambda c: jnp.maximum(nc - 1 - c, half)
    cur = lambda pos: pl.BlockSpec((1, CHUNK, CONV_CH), lambda bi, c: (bi, pos(c), 0))
    prev = lambda pos: pl.BlockSpec((1, SUBLANES, CONV_CH),
                                    lambda bi, c: (bi, jnp.maximum(pos(c) * hb - 1, 0), 0))
    nxt = lambda pos: pl.BlockSpec((1, SUBLANES, CONV_CH),
                                   lambda bi, c: (bi, jnp.minimum((pos(c) + 1) * hb, nhb - 1), 0))
    chunk_f = lambda n: pl.BlockSpec((1, CHUNK, n), lambda bi, c: (bi, c, 0))
    chunk_b = lambda n: pl.BlockSpec((1, CHUNK, n), lambda bi, c: (bi, nc - 1 - c, 0))
    return pl.pallas_call(
        functools.partial(_ssd_kernel, nc),
        grid=(b, nc),
        in_specs=[cur(fpos), prev(fpos), nxt(fpos), cur(bpos), prev(bpos), nxt(bpos),
                  chunk_f(LANES), chunk_b(LANES), chunk_f(SSD_W), chunk_b(SSD_W),
                  _full(conv_w.shape), _full(conv_b.shape), _full(bias.shape), _full(alog.shape),
                  _full(alog_x.shape), _full(tri.shape), _full(tri.shape), _full(expand.shape),
                  _full(dskip.shape), _full(nw.shape)],
        out_specs=pl.BlockSpec((1, s, SSD_W), lambda bi, c: (bi, 0, 0)),
        out_shape=jax.ShapeDtypeStruct((b, s, SSD_W), BF16),
        scratch_shapes=[pltpu.VMEM((s, CONV_CH), F32),
                        pltpu.VMEM((s, SSD_W), F32),
                        pltpu.VMEM((2, SSD_GROUPS, D_STATE, GROUP_W), F32)],
        compiler_params=_params(2),
        name="ssd",
    )(xbc, xbc, xbc, xbc, xbc, xbc, dt_raw, dt_raw, z, z, conv_w, conv_b, bias, alog, alog_x, tri, tri.T,
      expand, dskip, nw)


def _pack_bf16_pairs(v):
    n = v.shape[1] // 2
    hi = pltpu.bitcast(v[:, :n].astype(BF16).astype(F32), jnp.uint32)
    lo = pltpu.bitcast(v[:, n:].astype(BF16).astype(F32), jnp.uint32)
    return hi | (lo >> 16)


def _unpack_bf16_pairs(u):
    hi = pltpu.bitcast(u & jnp.uint32(0xFFFF0000), F32)
    lo = pltpu.bitcast(u << 16, F32)
    return hi, lo


def _store_packed(v, refs):
    words = _pack_bf16_pairs(v)
    w = refs[0].shape[-1]
    for j, ref in enumerate(refs):
        ref[...] = words[:, j * w:(j + 1) * w]


def _load_packed(refs):
    parts = [_unpack_bf16_pairs(ref[...]) for ref in refs]
    return jnp.concatenate([p[0] for p in parts] + [p[1] for p in parts], axis=-1)


R_E1, R_E2, R_C1, R_C2, R_RANK1, R_RANK2 = range(6)


def _out_proj_kernel(x_ref, ssd_ref, attn_ref, g_ref, wso_ref, wao_ref, wout_ref,
                     fnw_ref, wr_ref, br_ref, tril_ref, x2_ref, h2a_ref, h2b_ref, route_ref, cnt_ref,
                     base_ref):
    @pl.when(pl.program_id(0) == 0)
    def _():
        base_ref[...] = jnp.zeros_like(base_ref)

    g = g_ref[...]
    dm = x_ref.shape[1]
    merged = g[:, :dm] * _dot(attn_ref[...], wao_ref[...]) + g[:, dm:] * _dot(ssd_ref[...], wso_ref[...])
    x2 = x_ref[...] + _dot(merged.astype(BF16), wout_ref[...])
    x2_ref[...] = x2

    ms2 = jnp.mean(x2 * x2, axis=-1, keepdims=True)
    h2 = x2 * lax.rsqrt(ms2 + EPS) * fnw_ref[...]
    h2_hi = h2.astype(BF16)
    _store_packed(h2, (h2a_ref, h2b_ref))

    h2_lo = (h2 - h2_hi.astype(F32)).astype(BF16)
    pp = _dot(h2_hi, wr_ref[...]) + _dot(h2_lo, wr_ref[...])
    lg = pp[:, :LANES] + pp[:, LANES:] + br_ref[...]
    lane = lax.broadcasted_iota(jnp.int32, lg.shape, 1)
    neg = -jnp.inf
    big = jnp.int32(LANES)
    is_g = (lane >= N_EXPERTS) & (lane < N_EXPERTS + N_EXPERT_GROUPS)
    gl = jnp.where(is_g, lg, neg)
    ge = jnp.exp(gl - jnp.max(gl, axis=-1, keepdims=True))
    pg = ge / jnp.sum(ge, axis=-1, keepdims=True)
    g_val = jnp.max(pg, axis=-1, keepdims=True)
    g_idx = jnp.min(jnp.where(is_g & (pg == g_val), lane, big), axis=-1, keepdims=True) - N_EXPERTS
    lo = g_idx * EXPERTS_PER_GROUP
    sel = (lane >= lo) & (lane < lo + EXPERTS_PER_GROUP)
    fl = jnp.where(sel, lg, neg)
    fe = jnp.exp(fl - jnp.max(fl, axis=-1, keepdims=True))
    pf = fe / jnp.sum(fe, axis=-1, keepdims=True)
    v1 = jnp.max(pf, axis=-1, keepdims=True)
    i1 = jnp.min(jnp.where(sel & (pf == v1), lane, big), axis=-1, keepdims=True)
    pf2 = jnp.where(sel & (lane != i1), pf, -1.0)
    v2 = jnp.max(pf2, axis=-1, keepdims=True)
    i2 = jnp.min(jnp.where(pf2 == v2, lane, big), axis=-1, keepdims=True)
    den = v1 + v2
    c1 = g_val * (v1 / den)
    c2 = g_val * (v2 / den)

    oh = jnp.where(lane == i1, 1.0, 0.0) + jnp.where(lane == i2, 1.0, 0.0)
    cnt = _dot(tril_ref[...], oh.astype(BF16)) + base_ref[...]
    r1 = jnp.sum(jnp.where(lane == i1, cnt, 0.0), axis=-1, keepdims=True)
    r2 = jnp.sum(jnp.where(lane == i2, cnt, 0.0), axis=-1, keepdims=True)
    base = base_ref[...] + jnp.sum(oh, axis=0, keepdims=True)
    base_ref[...] = base
    cnt_ref[...] = jnp.broadcast_to(base, cnt_ref.shape)

    rec = jnp.zeros(lg.shape, F32)
    for slot, val in ((R_E1, i1.astype(F32)), (R_E2, i2.astype(F32)), (R_C1, c1), (R_C2, c2),
                      (R_RANK1, r1), (R_RANK2, r2)):
        rec = jnp.where(lane == slot, val, rec)
    route_ref[...] = rec


def _out_proj(x2d, ssd, attn, g, wso, wao, wout, fnw, wr, br):
    t, d = x2d.shape
    tm = TM_PROJ
    row = lambda n: pl.BlockSpec((tm, n), lambda i: (i, 0))
    tril_strict = jnp.tril(jnp.ones((tm, tm), BF16), k=-1)
    return pl.pallas_call(
        _out_proj_kernel,
        grid=(t // tm,),
        in_specs=[row(d), row(SSD_W), row(ATTN_W), row(2 * d),
                  _full(wso.shape), _full(wao.shape), _full(wout.shape),
                  _full(fnw.shape), _full(wr.shape), _full(br.shape), _full(tril_strict.shape)],
        out_specs=[row(d), row(SC_ROW_WORDS), row(SC_ROW_WORDS), row(LANES), _full((SUBLANES, LANES))],
        out_shape=[jax.ShapeDtypeStruct((t, d), F32),
                   jax.ShapeDtypeStruct((t, SC_ROW_WORDS), jnp.uint32),
                   jax.ShapeDtypeStruct((t, SC_ROW_WORDS), jnp.uint32),
                   jax.ShapeDtypeStruct((t, LANES), F32),
                   jax.ShapeDtypeStruct((SUBLANES, LANES), F32)],
        scratch_shapes=[pltpu.VMEM((1, LANES), F32)],
        compiler_params=_params(1),
        name="out_proj",
    )(x2d, ssd, attn, g, wso, wao, wout, fnw, wr, br, tril_strict)


def _sc_mesh():
    return plsc.VectorSubcoreMesh(core_axis_name="core", subcore_axis_name="subcore")


def _sc_scatter_rows(rows, idx_a, idx_b, n_out):
    n, w = rows.shape

    @pl.kernel(out_type=jax.ShapeDtypeStruct((n_out, w), rows.dtype), mesh=_sc_mesh(), name="moe_scatter")
    def scatter(x_hbm, ia_hbm, ib_hbm, o_hbm):
        def body(x_vmem, ia_vmem, ib_vmem):
            pltpu.sync_copy(x_vmem, o_hbm.at[ia_vmem.at[0]])
            pltpu.sync_copy(x_vmem, o_hbm.at[ib_vmem.at[0]])

        pltpu.emit_pipeline(
            body, grid=(n // SC_WINDOW,),
            in_specs=[pl.BlockSpec((SC_WINDOW, w), lambda i: (i, 0)),
                      pl.BlockSpec((1, SC_WINDOW), lambda i: (0, i)),
                      pl.BlockSpec((1, SC_WINDOW), lambda i: (0, i))],
            out_specs=[],
            core_axis_name=("core", "subcore"),
            dimension_semantics=(pltpu.PARALLEL,),
        )(x_hbm, ia_hbm, ib_hbm)

    return scatter(rows, idx_a.reshape(1, n), idx_b.reshape(1, n))


def _sc_gather_rows(table, idx):
    m = idx.shape[0]
    w = table.shape[1]

    @pl.kernel(out_type=jax.ShapeDtypeStruct((m, w), table.dtype), mesh=_sc_mesh(), name="moe_gather")
    def gather(x_hbm, i_hbm, o_hbm):
        def body(i_vmem, o_vmem):
            pltpu.sync_copy(x_hbm.at[i_vmem.at[0]], o_vmem)

        pltpu.emit_pipeline(
            body, grid=(m // SC_WINDOW,),
            in_specs=[pl.BlockSpec((1, SC_WINDOW), lambda i: (0, i))],
            out_specs=[pl.BlockSpec((SC_WINDOW, w), lambda i: (i, 0))],
            core_axis_name=("core", "subcore"),
            dimension_semantics=(pltpu.PARALLEL,),
        )(i_hbm, o_hbm)

    return gather(table, idx.reshape(1, m))


def _gmm_kernel(te_ref, nt_ref, xa_ref, xb_ref, w1_ref, w3_ref, w2_ref, ya_ref, yb_ref):
    @pl.when(pl.program_id(0) < nt_ref[0])
    def _():
        x = _load_packed((xa_ref, xb_ref)).astype(BF16)
        a = _dot(x, w1_ref[0])
        act = ((a * _sigmoid(a)) * _dot(x, w3_ref[0])).astype(BF16)
        _store_packed(_dot(act, w2_ref[0]), (ya_ref, yb_ref))


def _gmm(tile_expert, n_tiles, xs_a, xs_b, w1, w3, w2):
    p, words = xs_a.shape
    tm = TM_MOE
    d, de = w1.shape[1], w1.shape[2]
    row = pl.BlockSpec((tm, words), lambda i, te, nt: (jnp.minimum(i, nt[0] - 1), 0))
    packed = jax.ShapeDtypeStruct((p, words), jnp.uint32)
    return pl.pallas_call(
        _gmm_kernel,
        grid_spec=pltpu.PrefetchScalarGridSpec(
            num_scalar_prefetch=2,
            grid=(p // tm,),
            in_specs=[row, row,
                      pl.BlockSpec((1, d, de), lambda i, te, nt: (te[i], 0, 0)),
                      pl.BlockSpec((1, d, de), lambda i, te, nt: (te[i], 0, 0)),
                      pl.BlockSpec((1, de, d), lambda i, te, nt: (te[i], 0, 0))],
            out_specs=[row, row]),
        out_shape=[packed, packed],
        compiler_params=_params(1),
        name="moe_gmm",
    )(tile_expert, n_tiles, xs_a, xs_b, w1, w3, w2)


def _combine_kernel(x2_ref, y1a_ref, y1b_ref, y2a_ref, y2b_ref, route_ref, o_ref):
    route = route_ref[...]
    c1 = route[:, R_C1:R_C1 + 1]
    c2 = route[:, R_C2:R_C2 + 1]
    o_ref[...] = x2_ref[...] + (c1 * _load_packed((y1a_ref, y1b_ref)) + c2 * _load_packed((y2a_ref, y2b_ref)))


def _combine(x2, y12_a, y12_b, route):
    t, d = x2.shape
    tm = TM_MOE
    nb = t // tm
    words = y12_a.shape[1]
    first = pl.BlockSpec((tm, words), lambda i: (i, 0))
    second = pl.BlockSpec((tm, words), lambda i: (i + nb, 0))
    return pl.pallas_call(
        _combine_kernel,
        grid=(nb,),
        in_specs=[pl.BlockSpec((tm, d), lambda i: (i, 0)), first, first, second, second,
                  pl.BlockSpec((tm, LANES), lambda i: (i, 0))],
        out_specs=pl.BlockSpec((tm, d), lambda i: (i, 0)),
        out_shape=jax.ShapeDtypeStruct((t, d), F32),
        compiler_params=_params(1),
        name="moe_combine",
    )(x2, y12_a, y12_b, y12_a, y12_b, route)


def _routed_moe(x2, h2_a, h2_b, route, counts, w1, w3, w2):
    t = x2.shape[0]
    tm = TM_MOE
    p_max = 2 * t + N_EXPERTS * tm
    experts = jnp.arange(N_EXPERTS, dtype=jnp.int32)
    e12 = route[:, R_E1:R_E2 + 1].astype(jnp.int32)
    r12 = route[:, R_RANK1:R_RANK2 + 1].astype(jnp.int32)
    cnt = counts[0, :N_EXPERTS].astype(jnp.int32)
    padded = (cnt + tm - 1) // tm * tm
    ends = jnp.cumsum(padded)
    starts = ends - padded
    pos = jnp.sum(jnp.where(e12[..., None] == experts, starts, 0), axis=-1) + r12
    tile_ends = ends // tm
    tiles = jnp.arange(p_max // tm, dtype=jnp.int32)
    tile_expert = jnp.minimum(jnp.sum((tiles[:, None] >= tile_ends[None, :]).astype(jnp.int32), axis=1),
                              N_EXPERTS - 1)
    n_tiles = tile_ends[-1:].astype(jnp.int32)

    pos1, pos2 = pos[:, 0], pos[:, 1]
    xs_a = _sc_scatter_rows(h2_a, pos1, pos2, p_max)
    xs_b = _sc_scatter_rows(h2_b, pos1, pos2, p_max)
    y_a, y_b = _gmm(tile_expert, n_tiles, xs_a, xs_b, w1, w3, w2)
    pos12 = jnp.concatenate([pos1, pos2])
    return _combine(x2, _sc_gather_rows(y_a, pos12), _sc_gather_rows(y_b, pos12), route)


def _rope_tables(s):
    rows = s // GRID_W
    row = jnp.repeat(jnp.arange(rows, dtype=jnp.int32), GRID_W)
    col = jnp.tile(jnp.arange(GRID_W, dtype=jnp.int32), rows)
    half = HEAD_DIM // 2
    inv_freq = ROPE_THETA ** (-jnp.arange(0, half, 2, dtype=F32) / half)
    ang_r = row.astype(F32)[:, None] * inv_freq[None, :]
    ang_c = col.astype(F32)[:, None] * inv_freq[None, :]
    cos = jnp.concatenate([jnp.cos(ang_r), jnp.cos(ang_r), jnp.cos(ang_c), jnp.cos(ang_c)], axis=-1)
    sin_signed = jnp.concatenate([-jnp.sin(ang_r), jnp.sin(ang_r), -jnp.sin(ang_c), jnp.sin(ang_c)], axis=-1)
    return cos, sin_signed


def _pad_lanes(v):
    return jnp.pad(v, (0, LANES - v.shape[0]))[None]


def kernel(x, norm_mix_w, w_in, b_gate, q_norm_w, k_norm_w, w_attn_o, conv_w, conv_b, dt_bias, a_log, d_skip,
           ssd_norm_w, w_ssd_o, w_out, norm_ffn_w, w_router_group, b_router_group, w_router_expert,
           b_router_expert, w1, w3, w2):
    b, s, d = x.shape
    t = b * s
    depth = norm_mix_w.shape[0]
    cos, sin_signed = _rope_tables(s)
    x2d = x.reshape(t, d)
    sizes = (ATTN_W, KV_W, KV_W, SSD_W, CONV_CH, 2 * SSD_HEADS, 2 * d)
    offs = [0]
    for n in sizes:
        offs.append(offs[-1] + n)

    for l in range(depth):
        wi = w_in[l].astype(BF16)
        wq, wk, wv, wz, wxbc, wdt, wg = (wi[:, offs[j]:offs[j + 1]] for j in range(7))
        wdt = jnp.pad(wdt, ((0, 0), (0, LANES - 2 * SSD_HEADS)))
        q, k, v, z, xbc, dt_raw, g = _in_proj(x2d, norm_mix_w[l][None], wq, wk, wv, wz, wxbc, wdt, wg,
                                              b_gate[l][None])

        attn = _attention(q.reshape(b, s, ATTN_W), k, v, cos, sin_signed, q_norm_w[l][None], k_norm_w[l][None],
                          b, s)

        cw = jnp.pad(conv_w[l], ((0, SUBLANES - D_CONV), (0, 0)))
        ssd = _ssd(xbc.reshape(b, s, CONV_CH), dt_raw.reshape(b, s, LANES), z.reshape(b, s, SSD_W),
                   cw, conv_b[l][None], _pad_lanes(dt_bias[l].reshape(-1)), _pad_lanes(a_log[l].reshape(-1)),
                   jnp.repeat(a_log[l], SSD_HEAD_DIM, axis=1)[:, None, :],
                   jnp.repeat(d_skip[l], SSD_HEAD_DIM)[None], ssd_norm_w[l][None], b, s)

        wr = jnp.concatenate([w_router_expert[l], w_router_group[l]], axis=1)
        wr = jnp.pad(wr, ((0, 0), (0, LANES - wr.shape[1])))
        wr_hi = wr.astype(BF16)
        wr_lo = (wr - wr_hi.astype(F32)).astype(BF16)
        br = _pad_lanes(jnp.concatenate([b_router_expert[l], b_router_group[l]]))
        x2, h2_a, h2_b, route, counts = _out_proj(
            x2d, ssd.reshape(t, SSD_W), attn.reshape(t, ATTN_W), g,
            w_ssd_o[l].astype(BF16), w_attn_o[l].astype(BF16), w_out[l].astype(BF16),
            norm_ffn_w[l][None], jnp.concatenate([wr_hi, wr_lo], axis=1), br)

        x2d = _routed_moe(x2, h2_a, h2_b, route, counts,
                          w1[l].astype(BF16), w3[l].astype(BF16), w2[l].astype(BF16))
    return x2d.reshape(b, s, d)
```

```python
import functools
import math

import jax
import jax.numpy as jnp
from jax import lax
from jax.experimental import pallas as pl
from jax.experimental.pallas import tpu as pltpu
from jax.experimental.pallas import tpu_sc as plsc

GRID_W = 64
HEAD_DIM = 64
N_Q_HEADS = 8
N_KV_HEADS = 2
GQA_GROUP = N_Q_HEADS // N_KV_HEADS
ATTN_W = N_Q_HEADS * HEAD_DIM
KV_W = N_KV_HEADS * HEAD_DIM
ROPE_THETA = 10000.0
SSD_HEAD_DIM = 64
SSD_HEADS = 8
SSD_W = SSD_HEADS * SSD_HEAD_DIM
SSD_GROUPS = 2
HEADS_PER_GROUP = SSD_HEADS // SSD_GROUPS
GROUP_W = HEADS_PER_GROUP * SSD_HEAD_DIM
D_STATE = 64
D_CONV = 7
CONV_PAD = (D_CONV - 1) // 2
CONV_CH = SSD_W + 2 * SSD_GROUPS * D_STATE
CHUNK = 128
N_EXPERT_GROUPS = 4
EXPERTS_PER_GROUP = 4
N_EXPERTS = N_EXPERT_GROUPS * EXPERTS_PER_GROUP
EPS = 1e-6

LANES = 128
SUBLANES = 8
VMEM_LIMIT = 56 * 1024 * 1024

SC_WINDOW = 128
SC_ROW_WORDS = 256

TM_PROJ = 512
TQ = 256
TK = 4096
QCOLS = 1024
TM_MOE = 512

F32 = jnp.float32
BF16 = jnp.bfloat16
HIGHEST = lax.Precision.HIGHEST


def _dot(a, b):
    return jnp.dot(a, b, preferred_element_type=F32)


def _dot_nt(a, b):
    return lax.dot_general(a, b, (((1,), (1,)), ((), ())), preferred_element_type=F32)


def _dot_tn(a, b):
    return lax.dot_general(a, b, (((0,), (0,)), ((), ())), preferred_element_type=F32)


def _sigmoid(x):
    return 1.0 / (1.0 + jnp.exp(-x))


def _softplus(x):
    return jnp.maximum(x, 0.0) + jnp.log1p(jnp.exp(-jnp.abs(x)))


def _params(n_axes):
    return pltpu.CompilerParams(dimension_semantics=("arbitrary",) * n_axes,
                                vmem_limit_bytes=VMEM_LIMIT)


def _full(shape):
    return pl.BlockSpec(shape, lambda *_: (0,) * len(shape))


def _in_proj_kernel(x_ref, nw_ref, wq_ref, wk_ref, wv_ref, wz_ref, wxbc_ref, wdt_ref, wg_ref, bg_ref,
                    q_ref, k_ref, v_ref, z_ref, xbc_ref, dt_ref, g_ref):
    x = x_ref[...]
    ms = jnp.mean(x * x, axis=-1, keepdims=True)
    h = (x * lax.rsqrt(ms + EPS) * nw_ref[...]).astype(BF16)
    q_ref[...] = _dot(h, wq_ref[...])
    k = _dot(h, wk_ref[...])
    v = _dot(h, wv_ref[...]).astype(BF16)
    for j in range(N_KV_HEADS):
        k_ref[j] = k[:, j * HEAD_DIM:(j + 1) * HEAD_DIM]
        v_ref[j] = v[:, j * HEAD_DIM:(j + 1) * HEAD_DIM]
    z_ref[...] = _dot(h, wz_ref[...])
    xbc_ref[...] = _dot(h, wxbc_ref[...])
    dt_ref[...] = _dot(h, wdt_ref[...])
    g_ref[...] = _sigmoid(_dot(h, wg_ref[...]) + bg_ref[...]).astype(BF16)


def _in_proj(x2d, nw, wq, wk, wv, wz, wxbc, wdt, wg, bg):
    t, d = x2d.shape
    tm = TM_PROJ
    row = lambda n: pl.BlockSpec((tm, n), lambda i: (i, 0))
    return pl.pallas_call(
        _in_proj_kernel,
        grid=(t // tm,),
        in_specs=[row(d), _full(nw.shape), _full(wq.shape), _full(wk.shape), _full(wv.shape),
                  _full(wz.shape), _full(wxbc.shape), _full(wdt.shape), _full(wg.shape), _full(bg.shape)],
        out_specs=[row(ATTN_W),
                   pl.BlockSpec((N_KV_HEADS, tm, HEAD_DIM), lambda i: (0, i, 0)),
                   pl.BlockSpec((N_KV_HEADS, tm, HEAD_DIM), lambda i: (0, i, 0)),
                   row(SSD_W), row(CONV_CH), row(LANES), row(wg.shape[1])],
        out_shape=[jax.ShapeDtypeStruct((t, ATTN_W), F32),
                   jax.ShapeDtypeStruct((N_KV_HEADS, t, HEAD_DIM), F32),
                   jax.ShapeDtypeStruct((N_KV_HEADS, t, HEAD_DIM), BF16),
                   jax.ShapeDtypeStruct((t, SSD_W), F32),
                   jax.ShapeDtypeStruct((t, CONV_CH), F32),
                   jax.ShapeDtypeStruct((t, LANES), F32),
                   jax.ShapeDtypeStruct((t, wg.shape[1]), BF16)],
        compiler_params=_params(1),
        name="in_proj",
    )(x2d, nw, wq, wk, wv, wz, wxbc, wdt, wg, bg)


def _rope_partner(u):
    q = HEAD_DIM // 4
    return jnp.concatenate([u[:, q:2 * q], u[:, 0:q], u[:, 3 * q:4 * q], u[:, 2 * q:3 * q]], axis=-1)


def _norm_rope(u, w, cos, sin_signed):
    ms = jnp.mean(u * u, axis=-1, keepdims=True)
    un = u * lax.rsqrt(ms + EPS) * w
    return un * cos + _rope_partner(un) * sin_signed


def _transpose_narrow(x):
    if x.shape[1] == HEAD_DIM:
        return jnp.transpose(jnp.concatenate([x, jnp.zeros_like(x)], axis=1))[:HEAD_DIM, :]
    assert x.shape[0] == HEAD_DIM
    return jnp.transpose(jnp.concatenate([x, jnp.zeros_like(x)], axis=0))[:, :HEAD_DIM]


def _attend(qt, ks_ref, vt_ref):
    n = qt.shape[1]
    m = jnp.full((1, n), -jnp.inf, F32)
    l = jnp.zeros((1, n), F32)
    acc = jnp.zeros((HEAD_DIM, n), F32)
    tk = min(TK, ks_ref.shape[0])
    for kb in range(ks_ref.shape[0] // tk):
        keys = slice(kb * tk, (kb + 1) * tk)
        s = _dot(ks_ref[keys, :], qt)
        m_new = jnp.maximum(m, jnp.max(s, axis=0, keepdims=True))
        alpha = jnp.exp2(m - m_new)
        p = jnp.exp2(s - m_new)
        l = l * alpha + jnp.sum(p, axis=0, keepdims=True)
        acc = acc * alpha + _dot(vt_ref[:, keys], p.astype(BF16))
        m = m_new
    return acc / l


def _attn_kernel(q_ref, k_ref, v_ref, cosq_ref, sinq_ref, cosk_ref, sink_ref, qw_ref, kw_ref,
                 o_ref, ks_ref, vt_ref):
    @pl.when(pl.program_id(2) == 0)
    def _():
        ks_ref[...] = _norm_rope(k_ref[0], kw_ref[...], cosk_ref[...], sink_ref[...]).astype(BF16)
        vt_ref[...] = _transpose_narrow(v_ref[0].astype(F32)).astype(BF16)

    q = q_ref[0]
    tq = q.shape[0]
    cos = cosq_ref[...]
    sin = sinq_ref[...]
    scale = math.log2(math.e) / math.sqrt(HEAD_DIM)
    qt = jnp.concatenate(
        [_transpose_narrow(_norm_rope(q[:, g * HEAD_DIM:(g + 1) * HEAD_DIM], qw_ref[...], cos, sin) * scale)
         for g in range(GQA_GROUP)], axis=-1).astype(BF16)
    o = jnp.concatenate([_attend(qt[:, c:c + QCOLS], ks_ref, vt_ref) for c in range(0, qt.shape[1], QCOLS)],
                        axis=-1)
    o_ref[0] = jnp.concatenate(
        [_transpose_narrow(o[:, g * tq:(g + 1) * tq]) for g in range(GQA_GROUP)], axis=-1).astype(BF16)


def _attention(q, k, v, cos, sin_signed, qw, kw, b, s):
    tq = TQ
    gw = GQA_GROUP * HEAD_DIM
    return pl.pallas_call(
        _attn_kernel,
        grid=(b, N_KV_HEADS, s // tq),
        in_specs=[pl.BlockSpec((1, tq, gw), lambda bi, kv, qi: (bi, qi, kv)),
                  pl.BlockSpec((1, s, HEAD_DIM), lambda bi, kv, qi: (kv, bi, 0)),
                  pl.BlockSpec((1, s, HEAD_DIM), lambda bi, kv, qi: (kv, bi, 0)),
                  pl.BlockSpec((tq, HEAD_DIM), lambda bi, kv, qi: (qi, 0)),
                  pl.BlockSpec((tq, HEAD_DIM), lambda bi, kv, qi: (qi, 0)),
                  _full((s, HEAD_DIM)), _full((s, HEAD_DIM)),
                  _full((1, HEAD_DIM)), _full((1, HEAD_DIM))],
        out_specs=pl.BlockSpec((1, tq, gw), lambda bi, kv, qi: (bi, qi, kv)),
        out_shape=jax.ShapeDtypeStruct((b, s, ATTN_W), BF16),
        scratch_shapes=[pltpu.VMEM((s, HEAD_DIM), BF16), pltpu.VMEM((HEAD_DIM, s), BF16)],
        compiler_params=_params(3),
        name="attention",
    )(q, k, v, cos, sin_signed, cos, sin_signed, qw, kw)


def _conv_silu(cur_ref, prev_ref, next_ref, cw_ref, cb_ref, has_prev, has_next):
    prev = jnp.where(has_prev, prev_ref[0], 0.0)
    nxt = jnp.where(has_next, next_ref[0], 0.0)
    xin = jnp.concatenate([prev, cur_ref[0], nxt], axis=0)
    acc = jnp.zeros((CHUNK, CONV_CH), F32) + cb_ref[...]
    for t in range(D_CONV):
        off = SUBLANES - CONV_PAD + t
        acc = acc + xin[off:off + CHUNK, :] * cw_ref[t:t + 1, :]
    return acc * _sigmoid(acc)


def _split3(a):
    a1 = a.astype(BF16)
    r1 = a - a1.astype(F32)
    a2 = r1.astype(BF16)
    a3 = (r1 - a2.astype(F32)).astype(BF16)
    return a1, a2, a3


def _scan_chunk(direction, xc, dtraw, bias, alog, alog_x, tri_incl, tri_mask, expand, state_ref):
    dt_all = _softplus(dtraw + bias)
    h0 = direction * SSD_HEADS
    dta_t = jnp.transpose(dt_all * (-jnp.exp(alog)))[h0:h0 + SSD_HEADS, :]
    u_r = jnp.dot(dta_t, tri_incl, precision=HIGHEST, preferred_element_type=F32)

    dt_x = sum(_dot(piece, expand) for piece in _split3(dt_all))
    dta_x = dt_x * (-jnp.exp(alog_x))
    tri16 = tri_mask.astype(BF16)
    u_x = sum(_dot(tri16, piece) for piece in _split3(dta_x))
    tot_x = jnp.sum(dta_x, axis=0, keepdims=True)

    xdt = xc[:, :SSD_W] * dt_x
    xdt16 = xdt.astype(BF16)
    xw16 = (xdt * jnp.exp(tot_x - u_x)).astype(BF16)
    e_u = jnp.exp(u_x)
    keep = jnp.exp(tot_x)
    ys = []
    for g in range(SSD_GROUPS):
        gcols = slice(g * GROUP_W, (g + 1) * GROUP_W)
        bg16 = xc[:, SSD_W + g * D_STATE:SSD_W + (g + 1) * D_STATE].astype(BF16)
        cg16 = xc[:, SSD_W + (SSD_GROUPS + g) * D_STATE:SSD_W + (SSD_GROUPS + g + 1) * D_STATE].astype(BF16)
        cb = _dot_nt(cg16, bg16)
        st = state_ref[direction, g]
        y_diag = []
        for hh in range(HEADS_PER_GROUP):
            h = g * HEADS_PER_GROUP + hh
            hcols = slice(h * SSD_HEAD_DIM, (h + 1) * SSD_HEAD_DIM)
            uc = u_x[:, h * SSD_HEAD_DIM:h * SSD_HEAD_DIM + 1]
            decay = jnp.exp(jnp.where(tri_mask > 0.0, uc - u_r[h:h + 1, :], -jnp.inf))
            y_diag.append(_dot((cb * decay).astype(BF16), xdt16[:, hcols]))
        ys.append(jnp.concatenate(y_diag, axis=-1) + _dot(cg16, st.astype(BF16)) * e_u[:, gcols])
        state_ref[direction, g] = st * keep[:, gcols] + _dot_tn(bg16, xw16[:, gcols])
    return jnp.concatenate(ys, axis=-1)


def _gate_norm(y, z, w):
    y = y * (z * _sigmoid(z))
    ms = jnp.mean(y * y, axis=-1, keepdims=True)
    return (y * lax.rsqrt(ms + EPS) * w).astype(BF16)


def _ssd_kernel(nc, curf_ref, prevf_ref, nextf_ref, curb_ref, prevb_ref, nextb_ref, dtf_ref, dtb_ref,
                zf_ref, zb_ref, cw_ref, cb_ref, bias_ref, alog_ref, alogx_ref, mlow_ref, mup_ref, exp_ref,
                dskip_ref, nw_ref, o_ref, xc_ref, yacc_ref, state_ref):
    c = pl.program_id(1)
    cbk = nc - 1 - c
    first_visit = c < nc // 2
    rows_f = pl.ds(pl.multiple_of(c * CHUNK, CHUNK), CHUNK)
    rows_b = pl.ds(pl.multiple_of(cbk * CHUNK, CHUNK), CHUNK)

    @pl.when(c == 0)
    def _():
        state_ref[...] = jnp.zeros_like(state_ref)

    @pl.when(first_visit)
    def _():
        xc_ref[rows_f, :] = _conv_silu(curf_ref, prevf_ref, nextf_ref, cw_ref, cb_ref, c > 0, c < nc - 1)
        xc_ref[rows_b, :] = _conv_silu(curb_ref, prevb_ref, nextb_ref, cw_ref, cb_ref, cbk > 0, cbk < nc - 1)

    xcf = xc_ref[rows_f, :]
    xcb = xc_ref[rows_b, :]
    mlow = mlow_ref[...]
    mup = mup_ref[...]
    yf = _scan_chunk(0, xcf, dtf_ref[0], bias_ref[...], alog_ref[...], alogx_ref[0], mup, mlow, exp_ref[0],
                     state_ref)
    yb = _scan_chunk(1, xcb, dtb_ref[0], bias_ref[...], alog_ref[...], alogx_ref[1], mlow, mup, exp_ref[1],
                     state_ref)
    yf = yf + dskip_ref[...] * xcf[:, :SSD_W]

    @pl.when(first_visit)
    def _():
        yacc_ref[rows_f, :] = yf
        yacc_ref[rows_b, :] = yb

    @pl.when(jnp.logical_not(first_visit))
    def _():
        o_ref[0, rows_f, :] = _gate_norm(yacc_ref[rows_f, :] + yf, zf_ref[0], nw_ref[...])
        o_ref[0, rows_b, :] = _gate_norm(yacc_ref[rows_b, :] + yb, zb_ref[0], nw_ref[...])


def _ssd(xbc, dt_raw, z, conv_w, conv_b, bias, alog, alog_x, dskip, nw, b, s):
    nc = s // CHUNK
    assert nc % 2 == 0
    hb = CHUNK // SUBLANES
    nhb = s // SUBLANES
    half = nc // 2
    tri = jnp.tril(jnp.ones((CHUNK, CHUNK), F32))
    lane_head = jnp.arange(LANES)[None, :, None] - SSD_HEADS * jnp.arange(2)[:, None, None]
    expand = (lane_head == (jnp.arange(SSD_W) // SSD_HEAD_DIM)[None, None, :]).astype(BF16)
    fpos = lambda c: jnp.minimum(c, half - 1)
    bpos = lambda c: jnp.maximum(nc - 1 - c, half)
    cur = lambda pos: pl.BlockSpec((1, CHUNK, CONV_CH), lambda bi, c: (bi, pos(c), 0))
    prev = lambda pos: pl.BlockSpec((1, SUBLANES, CONV_CH),
                                    lambda bi, c: (bi, jnp.maximum(pos(c) * hb - 1, 0), 0))
    nxt = lambda pos: pl.BlockSpec((1, SUBLANES, CONV_CH),
                                   lambda bi, c: (bi, jnp.minimum((pos(c) + 1) * hb, nhb - 1), 0))
    chunk_f = lambda n: pl.BlockSpec((1, CHUNK, n), lambda bi, c: (bi, c, 0))
    chunk_b = lambda n: pl.BlockSpec((1, CHUNK, n), lambda bi, c: (bi, nc - 1 - c, 0))
    return pl.pallas_call(
        functools.partial(_ssd_kernel, nc),
        grid=(b, nc),
        in_specs=[cur(fpos), prev(fpos), nxt(fpos), cur(bpos), prev(bpos), nxt(bpos),
                  chunk_f(LANES), chunk_b(LANES), chunk_f(SSD_W), chunk_b(SSD_W),
                  _full(conv_w.shape), _full(conv_b.shape), _full(bias.shape), _full(alog.shape),
                  _full(alog_x.shape), _full(tri.shape), _full(tri.shape), _full(expand.shape),
                  _full(dskip.shape), _full(nw.shape)],
        out_specs=pl.BlockSpec((1, s, SSD_W), lambda bi, c: (bi, 0, 0)),
        out_shape=jax.ShapeDtypeStruct((b, s, SSD_W), BF16),
        scratch_shapes=[pltpu.VMEM((s, CONV_CH), F32),
                        pltpu.VMEM((s, SSD_W), F32),
                        pltpu.VMEM((2, SSD_GROUPS, D_STATE, GROUP_W), F32)],
        compiler_params=_params(2),
        name="ssd",
    )(xbc, xbc, xbc, xbc, xbc, xbc, dt_raw, dt_raw, z, z, conv_w, conv_b, bias, alog, alog_x, tri, tri.T,
      expand, dskip, nw)


def _pack_bf16_pairs(v):
    n = v.shape[1] // 2
    hi = pltpu.bitcast(v[:, :n].astype(BF16).astype(F32), jnp.uint32)
    lo = pltpu.bitcast(v[:, n:].astype(BF16).astype(F32), jnp.uint32)
    return hi | (lo >> 16)


def _unpack_bf16_pairs(u):
    hi = pltpu.bitcast(u & jnp.uint32(0xFFFF0000), F32)
    lo = pltpu.bitcast(u << 16, F32)
    return hi, lo


def _store_packed(v, refs):
    words = _pack_bf16_pairs(v)
    w = refs[0].shape[-1]
    for j, ref in enumerate(refs):
        ref[...] = words[:, j * w:(j + 1) * w]


def _load_packed(refs):
    parts = [_unpack_bf16_pairs(ref[...]) for ref in refs]
    return jnp.concatenate([p[0] for p in parts] + [p[1] for p in parts], axis=-1)


R_E1, R_E2, R_C1, R_C2, R_RANK1, R_RANK2 = range(6)


def _out_proj_kernel(x_ref, ssd_ref, attn_ref, g_ref, wso_ref, wao_ref, wout_ref,
                     fnw_ref, wr_ref, br_ref, tril_ref, x2_ref, h2a_ref, h2b_ref, route_ref, cnt_ref,
                     base_ref):
    @pl.when(pl.program_id(0) == 0)
    def _():
        base_ref[...] = jnp.zeros_like(base_ref)

    g = g_ref[...]
    dm = x_ref.shape[1]
    merged = g[:, :dm] * _dot(attn_ref[...], wao_ref[...]) + g[:, dm:] * _dot(ssd_ref[...], wso_ref[...])
    x2 = x_ref[...] + _dot(merged.astype(BF16), wout_ref[...])
    x2_ref[...] = x2

    ms2 = jnp.mean(x2 * x2, axis=-1, keepdims=True)
    h2 = x2 * lax.rsqrt(ms2 + EPS) * fnw_ref[...]
    h2_hi = h2.astype(BF16)
    _store_packed(h2, (h2a_ref, h2b_ref))

    h2_lo = (h2 - h2_hi.astype(F32)).astype(BF16)
    pp = _dot(h2_hi, wr_ref[...]) + _dot(h2_lo, wr_ref[...])
    lg = pp[:, :LANES] + pp[:, LANES:] + br_ref[...]
    lane = lax.broadcasted_iota(jnp.int32, lg.shape, 1)
    neg = -jnp.inf
    big = jnp.int32(LANES)
    is_g = (lane >= N_EXPERTS) & (lane < N_EXPERTS + N_EXPERT_GROUPS)
    gl = jnp.where(is_g, lg, neg)
    ge = jnp.exp(gl - jnp.max(gl, axis=-1, keepdims=True))
    pg = ge / jnp.sum(ge, axis=-1, keepdims=True)
    g_val = jnp.max(pg, axis=-1, keepdims=True)
    g_idx = jnp.min(jnp.where(is_g & (pg == g_val), lane, big), axis=-1, keepdims=True) - N_EXPERTS
    lo = g_idx * EXPERTS_PER_GROUP
    sel = (lane >= lo) & (lane < lo + EXPERTS_PER_GROUP)
    fl = jnp.where(sel, lg, neg)
    fe = jnp.exp(fl - jnp.max(fl, axis=-1, keepdims=True))
    pf = fe / jnp.sum(fe, axis=-1, keepdims=True)
    v1 = jnp.max(pf, axis=-1, keepdims=True)
    i1 = jnp.min(jnp.where(sel & (pf == v1), lane, big), axis=-1, keepdims=True)
    pf2 = jnp.where(sel & (lane != i1), pf, -1.0)
    v2 = jnp.max(pf2, axis=-1, keepdims=True)
    i2 = jnp.min(jnp.where(pf2 == v2, lane, big), axis=-1, keepdims=True)
    den = v1 + v2
    c1 = g_val * (v1 / den)
    c2 = g_val * (v2 / den)

    oh = jnp.where(lane == i1, 1.0, 0.0) + jnp.where(lane == i2, 1.0, 0.0)
    cnt = _dot(tril_ref[...], oh.astype(BF16)) + base_ref[...]
    r1 = jnp.sum(jnp.where(lane == i1, cnt, 0.0), axis=-1, keepdims=True)
    r2 = jnp.sum(jnp.where(lane == i2, cnt, 0.0), axis=-1, keepdims=True)
    base = base_ref[...] + jnp.sum(oh, axis=0, keepdims=True)
    base_ref[...] = base
    cnt_ref[...] = jnp.broadcast_to(base, cnt_ref.shape)

    rec = jnp.zeros(lg.shape, F32)
    for slot, val in ((R_E1, i1.astype(F32)), (R_E2, i2.astype(F32)), (R_C1, c1), (R_C2, c2),
                      (R_RANK1, r1), (R_RANK2, r2)):
        rec = jnp.where(lane == slot, val, rec)
    route_ref[...] = rec


def _out_proj(x2d, ssd, attn, g, wso, wao, wout, fnw, wr, br):
    t, d = x2d.shape
    tm = TM_PROJ
    row = lambda n: pl.BlockSpec((tm, n), lambda i: (i, 0))
    tril_strict = jnp.tril(jnp.ones((tm, tm), BF16), k=-1)
    return pl.pallas_call(
        _out_proj_kernel,
        grid=(t // tm,),
        in_specs=[row(d), row(SSD_W), row(ATTN_W), row(2 * d),
                  _full(wso.shape), _full(wao.shape), _full(wout.shape),
                  _full(fnw.shape), _full(wr.shape), _full(br.shape), _full(tril_strict.shape)],
        out_specs=[row(d), row(SC_ROW_WORDS), row(SC_ROW_WORDS), row(LANES), _full((SUBLANES, LANES))],
        out_shape=[jax.ShapeDtypeStruct((t, d), F32),
                   jax.ShapeDtypeStruct((t, SC_ROW_WORDS), jnp.uint32),
                   jax.ShapeDtypeStruct((t, SC_ROW_WORDS), jnp.uint32),
                   jax.ShapeDtypeStruct((t, LANES), F32),
                   jax.ShapeDtypeStruct((SUBLANES, LANES), F32)],
        scratch_shapes=[pltpu.VMEM((1, LANES), F32)],
        compiler_params=_params(1),
        name="out_proj",
    )(x2d, ssd, attn, g, wso, wao, wout, fnw, wr, br, tril_strict)


def _sc_mesh():
    return plsc.VectorSubcoreMesh(core_axis_name="core", subcore_axis_name="subcore")


def _sc_scatter_rows(rows, idx_a, idx_b, n_out):
    n, w = rows.shape

    @pl.kernel(out_type=jax.ShapeDtypeStruct((n_out, w), rows.dtype), mesh=_sc_mesh(), name="moe_scatter")
    def scatter(x_hbm, ia_hbm, ib_hbm, o_hbm):
        def body(x_vmem, ia_vmem, ib_vmem):
            pltpu.sync_copy(x_vmem, o_hbm.at[ia_vmem.at[0]])
            pltpu.sync_copy(x_vmem, o_hbm.at[ib_vmem.at[0]])

        pltpu.emit_pipeline(
            body, grid=(n // SC_WINDOW,),
            in_specs=[pl.BlockSpec((SC_WINDOW, w), lambda i: (i, 0)),
                      pl.BlockSpec((1, SC_WINDOW), lambda i: (0, i)),
                      pl.BlockSpec((1, SC_WINDOW), lambda i: (0, i))],
            out_specs=[],
            core_axis_name=("core", "subcore"),
            dimension_semantics=(pltpu.PARALLEL,),
        )(x_hbm, ia_hbm, ib_hbm)

    return scatter(rows, idx_a.reshape(1, n), idx_b.reshape(1, n))


def _sc_gather_rows(table, idx):
    m = idx.shape[0]
    w = table.shape[1]

    @pl.kernel(out_type=jax.ShapeDtypeStruct((m, w), table.dtype), mesh=_sc_mesh(), name="moe_gather")
    def gather(x_hbm, i_hbm, o_hbm):
        def body(i_vmem, o_vmem):
            pltpu.sync_copy(x_hbm.at[i_vmem.at[0]], o_vmem)

        pltpu.emit_pipeline(
            body, grid=(m // SC_WINDOW,),
            in_specs=[pl.BlockSpec((1, SC_WINDOW), lambda i: (0, i))],
            out_specs=[pl.BlockSpec((SC_WINDOW, w), lambda i: (i, 0))],
            core_axis_name=("core", "subcore"),
            dimension_semantics=(pltpu.PARALLEL,),
        )(i_hbm, o_hbm)

    return gather(table, idx.reshape(1, m))


def _gmm_kernel(te_ref, nt_ref, xa_ref, xb_ref, w1_ref, w3_ref, w2_ref, ya_ref, yb_ref, w1s, w3s, w2s):
    i = pl.program_id(0)

    @pl.when(i < nt_ref[0])
    def _():
        @pl.when(jnp.logical_or(i == 0, te_ref[i] != te_ref[jnp.maximum(i - 1, 0)]))
        def _():
            w1s[...] = w1_ref[0].astype(BF16)
            w3s[...] = w3_ref[0].astype(BF16)
            w2s[...] = w2_ref[0].astype(BF16)

        x = _load_packed((xa_ref, xb_ref)).astype(BF16)
        a = _dot(x, w1s[...])
        act = ((a * _sigmoid(a)) * _dot(x, w3s[...])).astype(BF16)
        _store_packed(_dot(act, w2s[...]), (ya_ref, yb_ref))


def _gmm(tile_expert, n_tiles, xs_a, xs_b, w1, w3, w2):
    p, words = xs_a.shape
    tm = TM_MOE
    d, de = w1.shape[1], w1.shape[2]
    row = pl.BlockSpec((tm, words), lambda i, te, nt: (jnp.minimum(i, nt[0] - 1), 0))
    packed = jax.ShapeDtypeStruct((p, words), jnp.uint32)
    return pl.pallas_call(
        _gmm_kernel,
        grid_spec=pltpu.PrefetchScalarGridSpec(
            num_scalar_prefetch=2,
            grid=(p // tm,),
            in_specs=[row, row,
                      pl.BlockSpec((1, d, de), lambda i, te, nt: (te[i], 0, 0)),
                      pl.BlockSpec((1, d, de), lambda i, te, nt: (te[i], 0, 0)),
                      pl.BlockSpec((1, de, d), lambda i, te, nt: (te[i], 0, 0))],
            out_specs=[row, row],
            scratch_shapes=[pltpu.VMEM((d, de), BF16), pltpu.VMEM((d, de), BF16), pltpu.VMEM((de, d), BF16)]),
        out_shape=[packed, packed],
        compiler_params=_params(1),
        name="moe_gmm",
    )(tile_expert, n_tiles, xs_a, xs_b, w1, w3, w2)


def _combine_kernel(x2_ref, y1a_ref, y1b_ref, y2a_ref, y2b_ref, route_ref, o_ref):
    route = route_ref[...]
    c1 = route[:, R_C1:R_C1 + 1]
    c2 = route[:, R_C2:R_C2 + 1]
    o_ref[...] = x2_ref[...] + (c1 * _load_packed((y1a_ref, y1b_ref)) + c2 * _load_packed((y2a_ref, y2b_ref)))


def _combine(x2, y12_a, y12_b, route):
    t, d = x2.shape
    tm = TM_MOE
    nb = t // tm
    words = y12_a.shape[1]
    first = pl.BlockSpec((tm, words), lambda i: (i, 0))
    second = pl.BlockSpec((tm, words), lambda i: (i + nb, 0))
    return pl.pallas_call(
        _combine_kernel,
        grid=(nb,),
        in_specs=[pl.BlockSpec((tm, d), lambda i: (i, 0)), first, first, second, second,
                  pl.BlockSpec((tm, LANES), lambda i: (i, 0))],
        out_specs=pl.BlockSpec((tm, d), lambda i: (i, 0)),
        out_shape=jax.ShapeDtypeStruct((t, d), F32),
        compiler_params=_params(1),
        name="moe_combine",
    )(x2, y12_a, y12_b, y12_a, y12_b, route)


def _routed_moe(x2, h2_a, h2_b, route, counts, w1, w3, w2):
    t = x2.shape[0]
    tm = TM_MOE
    p_max = 2 * t + N_EXPERTS * tm
    experts = jnp.arange(N_EXPERTS, dtype=jnp.int32)
    e12 = route[:, R_E1:R_E2 + 1].astype(jnp.int32)
    r12 = route[:, R_RANK1:R_RANK2 + 1].astype(jnp.int32)
    cnt = counts[0, :N_EXPERTS].astype(jnp.int32)
    padded = (cnt + tm - 1) // tm * tm
    ends = jnp.cumsum(padded)
    starts = ends - padded
    pos = jnp.sum(jnp.where(e12[..., None] == experts, starts, 0), axis=-1) + r12
    tile_ends = ends // tm
    tiles = jnp.arange(p_max // tm, dtype=jnp.int32)
    tile_expert = jnp.minimum(jnp.sum((tiles[:, None] >= tile_ends[None, :]).astype(jnp.int32), axis=1),
                              N_EXPERTS - 1)
    n_tiles = tile_ends[-1:].astype(jnp.int32)

    pos1, pos2 = pos[:, 0], pos[:, 1]
    xs_a = _sc_scatter_rows(h2_a, pos1, pos2, p_max)
    xs_b = _sc_scatter_rows(h2_b, pos1, pos2, p_max)
    y_a, y_b = _gmm(tile_expert, n_tiles, xs_a, xs_b, w1, w3, w2)
    pos12 = jnp.concatenate([pos1, pos2])
    return _combine(x2, _sc_gather_rows(y_a, pos12), _sc_gather_rows(y_b, pos12), route)


def _rope_tables(s):
    rows = s // GRID_W
    row = jnp.repeat(jnp.arange(rows, dtype=jnp.int32), GRID_W)
    col = jnp.tile(jnp.arange(GRID_W, dtype=jnp.int32), rows)
    half = HEAD_DIM // 2
    inv_freq = ROPE_THETA ** (-jnp.arange(0, half, 2, dtype=F32) / half)
    ang_r = row.astype(F32)[:, None] * inv_freq[None, :]
    ang_c = col.astype(F32)[:, None] * inv_freq[None, :]
    cos = jnp.concatenate([jnp.cos(ang_r), jnp.cos(ang_r), jnp.cos(ang_c), jnp.cos(ang_c)], axis=-1)
    sin_signed = jnp.concatenate([-jnp.sin(ang_r), jnp.sin(ang_r), -jnp.sin(ang_c), jnp.sin(ang_c)], axis=-1)
    return cos, sin_signed


def _pad_lanes(v):
    return jnp.pad(v, (0, LANES - v.shape[0]))[None]


def kernel(x, norm_mix_w, w_in, b_gate, q_norm_w, k_norm_w, w_attn_o, conv_w, conv_b, dt_bias, a_log, d_skip,
           ssd_norm_w, w_ssd_o, w_out, norm_ffn_w, w_router_group, b_router_group, w_router_expert,
           b_router_expert, w1, w3, w2):
    b, s, d = x.shape
    t = b * s
    depth = norm_mix_w.shape[0]
    cos, sin_signed = _rope_tables(s)
    x2d = x.reshape(t, d)
    sizes = (ATTN_W, KV_W, KV_W, SSD_W, CONV_CH, 2 * SSD_HEADS, 2 * d)
    offs = [0]
    for n in sizes:
        offs.append(offs[-1] + n)

    for l in range(depth):
        wi = w_in[l].astype(BF16)
        wq, wk, wv, wz, wxbc, wdt, wg = (wi[:, offs[j]:offs[j + 1]] for j in range(7))
        wdt = jnp.pad(wdt, ((0, 0), (0, LANES - 2 * SSD_HEADS)))
        q, k, v, z, xbc, dt_raw, g = _in_proj(x2d, norm_mix_w[l][None], wq, wk, wv, wz, wxbc, wdt, wg,
                                              b_gate[l][None])

        attn = _attention(q.reshape(b, s, ATTN_W), k, v, cos, sin_signed, q_norm_w[l][None], k_norm_w[l][None],
                          b, s)

        cw = jnp.pad(conv_w[l], ((0, SUBLANES - D_CONV), (0, 0)))
        ssd = _ssd(xbc.reshape(b, s, CONV_CH), dt_raw.reshape(b, s, LANES), z.reshape(b, s, SSD_W),
                   cw, conv_b[l][None], _pad_lanes(dt_bias[l].reshape(-1)), _pad_lanes(a_log[l].reshape(-1)),
                   jnp.repeat(a_log[l], SSD_HEAD_DIM, axis=1)[:, None, :],
                   jnp.repeat(d_skip[l], SSD_HEAD_DIM)[None], ssd_norm_w[l][None], b, s)

        wr = jnp.concatenate([w_router_expert[l], w_router_group[l]], axis=1)
        wr = jnp.pad(wr, ((0, 0), (0, LANES - wr.shape[1])))
        wr_hi = wr.astype(BF16)
        wr_lo = (wr - wr_hi.astype(F32)).astype(BF16)
        br = _pad_lanes(jnp.concatenate([b_router_expert[l], b_router_group[l]]))
        x2, h2_a, h2_b, route, counts = _out_proj(
            x2d, ssd.reshape(t, SSD_W), attn.reshape(t, ATTN_W), g,
            w_ssd_o[l].astype(BF16), w_attn_o[l].astype(BF16), w_out[l].astype(BF16),
            norm_ffn_w[l][None], jnp.concatenate([wr_hi, wr_lo], axis=1), br)

        x2d = _routed_moe(x2, h2_a, h2_b, route, counts, w1[l], w3[l], w2[l])
    return x2d.reshape(b, s, d)
```

```python
import functools
import math

import jax
import jax.numpy as jnp
from jax import lax
from jax.experimental import pallas as pl
from jax.experimental.pallas import tpu as pltpu
from jax.experimental.pallas import tpu_sc as plsc

GRID_W = 64
HEAD_DIM = 64
N_Q_HEADS = 8
N_KV_HEADS = 2
GQA_GROUP = N_Q_HEADS // N_KV_HEADS
ATTN_W = N_Q_HEADS * HEAD_DIM
KV_W = N_KV_HEADS * HEAD_DIM
ROPE_THETA = 10000.0
SSD_HEAD_DIM = 64
SSD_HEADS = 8
SSD_W = SSD_HEADS * SSD_HEAD_DIM
SSD_GROUPS = 2
HEADS_PER_GROUP = SSD_HEADS // SSD_GROUPS
GROUP_W = HEADS_PER_GROUP * SSD_HEAD_DIM
D_STATE = 64
D_CONV = 7
CONV_PAD = (D_CONV - 1) // 2
CONV_CH = SSD_W + 2 * SSD_GROUPS * D_STATE
CHUNK = 128
N_EXPERT_GROUPS = 4
EXPERTS_PER_GROUP = 4
N_EXPERTS = N_EXPERT_GROUPS * EXPERTS_PER_GROUP
EPS = 1e-6

LANES = 128
SUBLANES = 8
VMEM_LIMIT = 56 * 1024 * 1024

SC_WINDOW = 128
SC_ROW_WORDS = 256

TM_PROJ = 512
TM_OUT = 1024
PROJ_SPLIT = 8
TQ = 512
KEY_BLOCK = 2048
BOUND_SHIFT_LIMIT = 50.0
BOUND_MARGIN = 1.02
SSD_BATCH = 2
TM_MOE = 512
MOE_SLABS = 2

F32 = jnp.float32
BF16 = jnp.bfloat16
HIGHEST = lax.Precision.HIGHEST


def _dot(a, b):
    return jnp.dot(a, b, preferred_element_type=F32)


def _dot_nt(a, b):
    return lax.dot_general(a, b, (((1,), (1,)), ((), ())), preferred_element_type=F32)


def _dot_tn(a, b):
    return lax.dot_general(a, b, (((0,), (0,)), ((), ())), preferred_element_type=F32)


def _sigmoid(x):
    return 1.0 / (1.0 + jnp.exp(-x))


def _softplus(x):
    return jnp.maximum(x, 0.0) + jnp.log1p(jnp.exp(-jnp.abs(x)))


def _params(n_axes, **flags):
    return pltpu.CompilerParams(dimension_semantics=("arbitrary",) * n_axes,
                                vmem_limit_bytes=VMEM_LIMIT, flags=flags or None)


def _full(shape):
    return pl.BlockSpec(shape, lambda *_: (0,) * len(shape))


def _in_proj_kernel(tiles_per_seq, x_ref, xprev_ref, xnext_ref, nw_ref, wrest_ref, wxbc_ref, bg_ref, cw_ref, cb_ref,
                    q_ref, k_ref, v_ref, z_ref, xc_ref, dt_ref, g_ref):
    tm = x_ref.shape[0]

    def normed(x):
        ms = jnp.mean(x * x, axis=-1, keepdims=True)
        return (x * lax.rsqrt(ms + EPS) * nw_ref[...]).astype(BF16)

    h = normed(x_ref[...])

    pos = pl.program_id(0) % tiles_per_seq
    h_halo = normed(jnp.concatenate([xprev_ref[...], xnext_ref[...]], axis=0))
    xbc = _dot(jnp.concatenate([h, h_halo], axis=0), wxbc_ref[...])
    prev = jnp.where(pos > 0, xbc[tm:tm + SUBLANES], 0.0)
    nxt = jnp.where(pos < tiles_per_seq - 1, xbc[tm + SUBLANES:], 0.0)
    xin = jnp.concatenate([prev, xbc[:tm], nxt], axis=0)
    acc = jnp.zeros((tm, CONV_CH), F32) + cb_ref[...]
    for t in range(D_CONV):
        off = SUBLANES - CONV_PAD + t
        acc = acc + xin[off:off + tm, :] * cw_ref[t:t + 1, :]
    xc_ref[...] = acc * _sigmoid(acc)

    r = _dot(h, wrest_ref[...])
    edges = [0]
    for width in (ATTN_W, KV_W, KV_W, SSD_W, g_ref.shape[1], LANES):
        edges.append(edges[-1] + width)
    q, k, v, z, g, dt = (r[:, a:b] for a, b in zip(edges[:-1], edges[1:]))
    q_ref[...] = q
    v = v.astype(BF16)
    for j in range(N_KV_HEADS):
        k_ref[j] = k[:, j * HEAD_DIM:(j + 1) * HEAD_DIM]
        v_ref[j] = v[:, j * HEAD_DIM:(j + 1) * HEAD_DIM]
    z_ref[...] = z
    dt_ref[...] = dt
    g_ref[...] = _sigmoid(g + bg_ref[...]).astype(BF16)


def _in_proj(x2d, nw, wrest, wxbc, bg, conv_w, conv_b, seq_len):
    t, d = x2d.shape
    tm = TM_PROJ
    hb = tm // SUBLANES
    row = lambda n: pl.BlockSpec((tm, n), lambda i: (i, 0))
    return pl.pallas_call(
        functools.partial(_in_proj_kernel, seq_len // tm),
        grid=(t // tm,),
        in_specs=[row(d),
                  pl.BlockSpec((SUBLANES, d), lambda i: (jnp.maximum(i * hb - 1, 0), 0)),
                  pl.BlockSpec((SUBLANES, d), lambda i: (jnp.minimum((i + 1) * hb, t // SUBLANES - 1), 0)),
                  _full(nw.shape), _full(wrest.shape), _full(wxbc.shape), _full(bg.shape),
                  _full(conv_w.shape), _full(conv_b.shape)],
        out_specs=[row(ATTN_W),
                   pl.BlockSpec((N_KV_HEADS, tm, HEAD_DIM), lambda i: (0, i, 0)),
                   pl.BlockSpec((N_KV_HEADS, tm, HEAD_DIM), lambda i: (0, i, 0)),
                   row(SSD_W), row(CONV_CH), row(LANES), row(bg.shape[1])],
        out_shape=[jax.ShapeDtypeStruct((t, ATTN_W), F32),
                   jax.ShapeDtypeStruct((N_KV_HEADS, t, HEAD_DIM), F32),
                   jax.ShapeDtypeStruct((N_KV_HEADS, t, HEAD_DIM), BF16),
                   jax.ShapeDtypeStruct((t, SSD_W), F32),
                   jax.ShapeDtypeStruct((t, CONV_CH), F32),
                   jax.ShapeDtypeStruct((t, LANES), F32),
                   jax.ShapeDtypeStruct((t, bg.shape[1]), BF16)],
        compiler_params=_params(1),
        name="in_proj",
    )(x2d, x2d, x2d, nw, wrest, wxbc, bg, conv_w, conv_b)


def _rope_partner(u):
    q = HEAD_DIM // 4
    return jnp.concatenate([u[:, q:2 * q], u[:, 0:q], u[:, 3 * q:4 * q], u[:, 2 * q:3 * q]], axis=-1)


def _norm_rope(u, w, cos, sin_signed):
    ms = jnp.mean(u * u, axis=-1, keepdims=True)
    un = u * lax.rsqrt(ms + EPS) * w
    return un * cos + _rope_partner(un) * sin_signed


def _transpose_narrow(x):
    if x.shape[1] == HEAD_DIM:
        return jnp.transpose(jnp.concatenate([x, jnp.zeros_like(x)], axis=1))[:HEAD_DIM, :]
    assert x.shape[0] == HEAD_DIM
    return jnp.transpose(jnp.concatenate([x, jnp.zeros_like(x)], axis=0))[:, :HEAD_DIM]


def _attn_kernel(q_ref, k_ref, v_ref, cosq_ref, sinq_ref, cosk_ref, sink_ref, qw_ref, kw_ref,
                 o_ref, ks_ref, vt_ref, acc_ref, l_ref):
    @pl.when(pl.program_id(2) == 0)
    def _():
        ks_ref[...] = _norm_rope(k_ref[0], kw_ref[...], cosk_ref[...], sink_ref[...]).astype(BF16)
        vt_ref[...] = _transpose_narrow(v_ref[0].astype(F32)).astype(BF16)

    tq = q_ref.shape[1]
    scale = math.log2(math.e) / math.sqrt(HEAD_DIM)
    q_t = jnp.transpose(q_ref[0])
    cos_t = cosq_ref[...]
    sin_t = sinq_ref[...]
    quarter = HEAD_DIM // 4
    heads = []
    for g in range(GQA_GROUP):
        u = q_t[g * HEAD_DIM:(g + 1) * HEAD_DIM, :]
        un = u * lax.rsqrt(jnp.mean(u * u, axis=0, keepdims=True) + EPS) * qw_ref[...]
        partner = jnp.concatenate([un[quarter:2 * quarter], un[:quarter],
                                   un[3 * quarter:], un[2 * quarter:3 * quarter]], axis=0)
        heads.append((un * cos_t + partner * sin_t) * scale)
    qt = jnp.concatenate(heads, axis=-1).astype(BF16)
    n_keys = ks_ref.shape[0]
    kb = min(KEY_BLOCK, n_keys)
    blocks = range(0, n_keys, kb)

    bound = (BOUND_MARGIN * scale * HEAD_DIM
             * jnp.max(jnp.abs(qw_ref[...])) * jnp.max(jnp.abs(kw_ref[...])))
    use_bound = bound < BOUND_SHIFT_LIMIT

    @pl.when(use_bound)
    def _():
        l = acc = None
        for k0 in blocks:
            p = jnp.exp2(_dot(ks_ref[k0:k0 + kb, :], qt) - bound)
            lj = jnp.sum(p, axis=0, keepdims=True)
            oj = _dot(vt_ref[:, k0:k0 + kb], p.astype(BF16))
            l, acc = (lj, oj) if l is None else (l + lj, acc + oj)
        l_ref[...] = l
        acc_ref[...] = acc

    @pl.when(jnp.logical_not(use_bound))
    def _():
        m = l = acc = None
        for k0 in blocks:
            s = _dot(ks_ref[k0:k0 + kb, :], qt)
            mj = jnp.max(s, axis=0, keepdims=True)
            m_new = mj if m is None else jnp.maximum(m, mj)
            p = jnp.exp2(s - m_new)
            lj = jnp.sum(p, axis=0, keepdims=True)
            oj = _dot(vt_ref[:, k0:k0 + kb], p.astype(BF16))
            if m is None:
                l, acc = lj, oj
            else:
                alpha = jnp.exp2(m - m_new)
                l, acc = l * alpha + lj, acc * alpha + oj
            m = m_new
        l_ref[...] = l
        acc_ref[...] = acc

    o = acc_ref[...] / l_ref[...]
    o_ref[0] = jnp.transpose(
        jnp.concatenate([o[:, g * tq:(g + 1) * tq] for g in range(GQA_GROUP)], axis=0)).astype(BF16)


def _attention(q, k, v, cos, sin_signed, qw, kw, b, s):
    tq = TQ
    gw = GQA_GROUP * HEAD_DIM
    return pl.pallas_call(
        _attn_kernel,
        grid=(b, N_KV_HEADS, s // tq),
        in_specs=[pl.BlockSpec((1, tq, gw), lambda bi, kv, qi: (bi, qi, kv)),
                  pl.BlockSpec((1, s, HEAD_DIM), lambda bi, kv, qi: (kv, bi, 0)),
                  pl.BlockSpec((1, s, HEAD_DIM), lambda bi, kv, qi: (kv, bi, 0)),
                  pl.BlockSpec((HEAD_DIM, tq), lambda bi, kv, qi: (0, qi)),
                  pl.BlockSpec((HEAD_DIM, tq), lambda bi, kv, qi: (0, qi)),
                  _full((s, HEAD_DIM)), _full((s, HEAD_DIM)),
                  _full((HEAD_DIM, 1)), _full((1, HEAD_DIM))],
        out_specs=pl.BlockSpec((1, tq, gw), lambda bi, kv, qi: (bi, qi, kv)),
        out_shape=jax.ShapeDtypeStruct((b, s, ATTN_W), BF16),
        scratch_shapes=[pltpu.VMEM((s, HEAD_DIM), BF16), pltpu.VMEM((HEAD_DIM, s), BF16),
                        pltpu.VMEM((HEAD_DIM, GQA_GROUP * tq), F32), pltpu.VMEM((1, GQA_GROUP * tq), F32)],
        compiler_params=_params(3),
        name="attention",
    )(q, k, v, cos.T, sin_signed.T, cos, sin_signed, qw.T, kw)


def _split3(a):
    a1 = a.astype(BF16)
    r1 = a - a1.astype(F32)
    a2 = r1.astype(BF16)
    a3 = (r1 - a2.astype(F32)).astype(BF16)
    return a1, a2, a3


def _scan_chunk(direction, xc, dtraw, bias, alog, alog_x, tri_incl, tri_mask, expand, states):
    dt_all = _softplus(dtraw + bias)
    h0 = direction * SSD_HEADS
    dta_t = jnp.transpose(dt_all * (-jnp.exp(alog)))[h0:h0 + SSD_HEADS, :]
    u_r = jnp.dot(dta_t, tri_incl, precision=HIGHEST, preferred_element_type=F32)
    yield
    dt_x = sum(_dot(piece, expand) for piece in _split3(dt_all))
    yield
    dta_x = dt_x * (-jnp.exp(alog_x))
    tri16 = tri_mask.astype(BF16)
    u_x = sum(_dot(tri16, piece) for piece in _split3(dta_x))
    tot_x = jnp.sum(dta_x, axis=0, keepdims=True)
    yield
    xdt = xc[:, :SSD_W] * dt_x
    xdt16 = xdt.astype(BF16)
    xw16 = (xdt * jnp.exp(tot_x - u_x)).astype(BF16)
    e_u = jnp.exp(u_x)
    keep = jnp.exp(tot_x)
    yield

    def group(g):
        gcols = slice(g * GROUP_W, (g + 1) * GROUP_W)
        bg16 = xc[:, SSD_W + g * D_STATE:SSD_W + (g + 1) * D_STATE].astype(BF16)
        cg16 = xc[:, SSD_W + (SSD_GROUPS + g) * D_STATE:SSD_W + (SSD_GROUPS + g + 1) * D_STATE].astype(BF16)
        cb = _dot_nt(cg16, bg16)
        st = states[g]
        yield
        y_diag = []
        for hh in range(HEADS_PER_GROUP):
            h = g * HEADS_PER_GROUP + hh
            hcols = slice(h * SSD_HEAD_DIM, (h + 1) * SSD_HEAD_DIM)
            uc = u_x[:, h * SSD_HEAD_DIM:h * SSD_HEAD_DIM + 1]
            decay = jnp.exp(jnp.where(tri_mask > 0.0, uc - u_r[h:h + 1, :], -jnp.inf))
            y_diag.append(_dot((cb * decay).astype(BF16), xdt16[:, hcols]))
            yield
        y = jnp.concatenate(y_diag, axis=-1) + _dot(cg16, st.astype(BF16)) * e_u[:, gcols]
        return y, st * keep[:, gcols] + _dot_tn(bg16, xw16[:, gcols])

    gens = [group(g) for g in range(SSD_GROUPS)]
    results = [None] * SSD_GROUPS
    live = list(range(SSD_GROUPS))
    while live:
        for g in list(live):
            try:
                next(gens[g])
            except StopIteration as done:
                results[g] = done.value
                live.remove(g)
            yield
    return jnp.concatenate([r[0] for r in results], axis=-1), [r[1] for r in results]


def _lockstep(*stages):
    results = [None] * len(stages)
    live = list(range(len(stages)))
    while live:
        for i in list(live):
            try:
                next(stages[i])
            except StopIteration as done:
                results[i] = done.value
                live.remove(i)
    return results


def _gate_norm(y, z, w):
    y = y * (z * _sigmoid(z))
    ms = jnp.mean(y * y, axis=-1, keepdims=True)
    return (y * lax.rsqrt(ms + EPS) * w).astype(BF16)


def _ssd_kernel(nc, xcf_ref, xcb_ref, dtf_ref, dtb_ref, zf_ref, zb_ref, bias_ref, alog_ref, alogx_ref,
                mlow_ref, mup_ref, exp_ref, dskip_ref, nw_ref, o_ref, yacc_ref, *state_refs):
    c = pl.program_id(1)
    cbk = nc - 1 - c
    first_visit = c < nc // 2
    rows_f = pl.ds(pl.multiple_of(c * CHUNK, CHUNK), CHUNK)
    rows_b = pl.ds(pl.multiple_of(cbk * CHUNK, CHUNK), CHUNK)

    @pl.when(c == 0)
    def _():
        for ref in state_refs:
            ref[...] = jnp.zeros_like(ref)

    mlow = mlow_ref[...]
    mup = mup_ref[...]
    nb = xcf_ref.shape[0]
    states = [ref[...] for ref in state_refs]
    per = 2 * SSD_GROUPS
    scans = []
    for i in range(nb):
        st = states[i * per:(i + 1) * per]
        scans.append(_scan_chunk(0, xcf_ref[i], dtf_ref[i], bias_ref[...], alog_ref[...], alogx_ref[0],
                                 mup, mlow, exp_ref[0], st[:SSD_GROUPS]))
        scans.append(_scan_chunk(1, xcb_ref[i], dtb_ref[i], bias_ref[...], alog_ref[...], alogx_ref[1],
                                 mlow, mup, exp_ref[1], st[SSD_GROUPS:]))
    results = _lockstep(*scans)
    for ref, st in zip(state_refs, [s_ for _, new in results for s_ in new]):
        ref[...] = st
    ys = [(results[2 * i][0] + dskip_ref[...] * xcf_ref[i][:, :SSD_W], results[2 * i + 1][0]) for i in range(nb)]

    @pl.when(first_visit)
    def _():
        for i, (yf, yb) in enumerate(ys):
            yacc_ref[i, rows_f, :] = yf
            yacc_ref[i, rows_b, :] = yb

    @pl.when(jnp.logical_not(first_visit))
    def _():
        for i, (yf, yb) in enumerate(ys):
            o_ref[i, rows_f, :] = _gate_norm(yacc_ref[i, rows_f, :] + yf, zf_ref[i], nw_ref[...])
            o_ref[i, rows_b, :] = _gate_norm(yacc_ref[i, rows_b, :] + yb, zb_ref[i], nw_ref[...])


def _ssd(xc, dt_raw, z, bias, alog, alog_x, dskip, nw, b, s):
    nc = s // CHUNK
    assert nc % 2 == 0
    tri = jnp.tril(jnp.ones((CHUNK, CHUNK), F32))
    lane_head = jnp.arange(LANES)[None, :, None] - SSD_HEADS * jnp.arange(2)[:, None, None]
    expand = (lane_head == (jnp.arange(SSD_W) // SSD_HEAD_DIM)[None, None, :]).astype(BF16)
    nb = SSD_BATCH if b % SSD_BATCH == 0 else 1
    chunk_f = lambda n: pl.BlockSpec((nb, CHUNK, n), lambda bi, c: (bi, c, 0))
    chunk_b = lambda n: pl.BlockSpec((nb, CHUNK, n), lambda bi, c: (bi, nc - 1 - c, 0))
    return pl.pallas_call(
        functools.partial(_ssd_kernel, nc),
        grid=(b // nb, nc),
        in_specs=[chunk_f(CONV_CH), chunk_b(CONV_CH), chunk_f(LANES), chunk_b(LANES),
                  chunk_f(SSD_W), chunk_b(SSD_W), _full(bias.shape), _full(alog.shape),
                  _full(alog_x.shape), _full(tri.shape), _full(tri.shape), _full(expand.shape),
                  _full(dskip.shape), _full(nw.shape)],
        out_specs=pl.BlockSpec((nb, s, SSD_W), lambda bi, c: (bi, 0, 0)),
        out_shape=jax.ShapeDtypeStruct((b, s, SSD_W), BF16),
        scratch_shapes=[pltpu.VMEM((nb, s, SSD_W), F32)]
                       + [pltpu.VMEM((D_STATE, GROUP_W), F32)] * (nb * 2 * SSD_GROUPS),
        compiler_params=_params(2),
        name="ssd",
    )(xc, xc, dt_raw, dt_raw, z, z, bias, alog, alog_x, tri, tri.T, expand, dskip, nw)


def _pack_bf16_pairs(v):
    n = v.shape[1] // 2
    hi = pltpu.bitcast(v[:, :n].astype(BF16).astype(F32), jnp.uint32)
    lo = pltpu.bitcast(v[:, n:].astype(BF16).astype(F32), jnp.uint32)
    return hi | (lo >> 16)


def _unpack_bf16_pairs(u):
    hi = pltpu.bitcast(u & jnp.uint32(0xFFFF0000), F32)
    lo = pltpu.bitcast(u << 16, F32)
    return hi, lo


def _store_packed(v, refs):
    words = _pack_bf16_pairs(v)
    w = refs[0].shape[-1]
    for j, ref in enumerate(refs):
        ref[...] = words[:, j * w:(j + 1) * w]


def _load_packed(refs):
    parts = [_unpack_bf16_pairs(ref[...]) for ref in refs]
    return jnp.concatenate([p[0] for p in parts] + [p[1] for p in parts], axis=-1)


R_E1, R_E2, R_C1, R_C2, R_RANK1, R_RANK2 = range(6)


def _out_proj_kernel(x_ref, ssd_ref, attn_ref, g_ref, wso_ref, wao_ref, wout_ref,
                     fnw_ref, wr_ref, br_ref, tril_ref, x2_ref, h2a_ref, h2b_ref, route_ref, route_t_ref,
                     cnt_ref, base_ref):
    @pl.when(pl.program_id(0) == 0)
    def _():
        base_ref[...] = jnp.zeros_like(base_ref)

    dm = x_ref.shape[1]
    rb = x_ref.shape[0] // PROJ_SPLIT

    def block(rows):
        g = g_ref[rows, :]
        merged = (g[:, :dm] * _dot(attn_ref[rows, :], wao_ref[...])
                  + g[:, dm:] * _dot(ssd_ref[rows, :], wso_ref[...]))
        yield
        x2 = x_ref[rows, :] + _dot(merged.astype(BF16), wout_ref[...])
        x2_ref[rows, :] = x2
        yield
        ms2 = jnp.mean(x2 * x2, axis=-1, keepdims=True)
        h2 = x2 * lax.rsqrt(ms2 + EPS) * fnw_ref[...]
        h2_hi = h2.astype(BF16)
        _store_packed(h2, (h2a_ref.at[rows], h2b_ref.at[rows]))
        yield
        h2_lo = (h2 - h2_hi.astype(F32)).astype(BF16)
        pp = _dot(h2_hi, wr_ref[...]) + _dot(h2_lo, wr_ref[...])
        lg = pp[:, :LANES] + pp[:, LANES:] + br_ref[...]
        yield
        lane = lax.broadcasted_iota(jnp.int32, lg.shape, 1)
        neg = -jnp.inf
        big = jnp.int32(LANES)
        is_g = (lane >= N_EXPERTS) & (lane < N_EXPERTS + N_EXPERT_GROUPS)
        gl = jnp.where(is_g, lg, neg)
        ge = jnp.exp(gl - jnp.max(gl, axis=-1, keepdims=True))
        pg = ge / jnp.sum(ge, axis=-1, keepdims=True)
        g_val = jnp.max(pg, axis=-1, keepdims=True)
        g_idx = jnp.min(jnp.where(is_g & (pg == g_val), lane, big), axis=-1, keepdims=True) - N_EXPERTS
        yield
        lo = g_idx * EXPERTS_PER_GROUP
        sel = (lane >= lo) & (lane < lo + EXPERTS_PER_GROUP)
        fl = jnp.where(sel, lg, neg)
        fe = jnp.exp(fl - jnp.max(fl, axis=-1, keepdims=True))
        pf = fe / jnp.sum(fe, axis=-1, keepdims=True)
        v1 = jnp.max(pf, axis=-1, keepdims=True)
        i1 = jnp.min(jnp.where(sel & (pf == v1), lane, big), axis=-1, keepdims=True)
        yield
        pf2 = jnp.where(sel & (lane != i1), pf, -1.0)
        v2 = jnp.max(pf2, axis=-1, keepdims=True)
        i2 = jnp.min(jnp.where(pf2 == v2, lane, big), axis=-1, keepdims=True)
        den = v1 + v2
        oh = jnp.where(lane == i1, 1.0, 0.0) + jnp.where(lane == i2, 1.0, 0.0)
        return lane, i1, i2, g_val * (v1 / den), g_val * (v2 / den), oh

    blocks = [pl.ds(r * rb, rb) for r in range(PROJ_SPLIT)]
    base = base_ref[...]
    for rows, (lane, i1, i2, c1, c2, oh) in zip(blocks, _lockstep(*[block(rows) for rows in blocks])):
        cnt = _dot(tril_ref[:rb, :rb], oh.astype(BF16)) + base
        r1 = jnp.sum(jnp.where(lane == i1, cnt, 0.0), axis=-1, keepdims=True)
        r2 = jnp.sum(jnp.where(lane == i2, cnt, 0.0), axis=-1, keepdims=True)
        base = base + jnp.sum(oh, axis=0, keepdims=True)
        rec = jnp.zeros(oh.shape, F32)
        for slot, val in ((R_E1, i1.astype(F32)), (R_E2, i2.astype(F32)), (R_C1, c1), (R_C2, c2),
                          (R_RANK1, r1), (R_RANK2, r2)):
            rec = jnp.where(lane == slot, val, rec)
        route_ref[rows, :] = rec
        route_t_ref[:, rows] = jnp.transpose(rec)[:SUBLANES, :]
    base_ref[...] = base
    cnt_ref[...] = jnp.broadcast_to(base, cnt_ref.shape)


def _out_proj(x2d, ssd, attn, g, wso, wao, wout, fnw, wr, br, slab, n_slabs):
    d = x2d.shape[1]
    t = x2d.shape[0] // n_slabs
    tm = min(TM_OUT, t)
    first = slab * (t // tm)
    row_in = lambda n: pl.BlockSpec((tm, n), lambda i: (i + first, 0))
    row = lambda n: pl.BlockSpec((tm, n), lambda i: (i, 0))
    tril_strict = jnp.tril(jnp.ones((tm // PROJ_SPLIT, tm // PROJ_SPLIT), BF16), k=-1)
    return pl.pallas_call(
        _out_proj_kernel,
        grid=(t // tm,),
        in_specs=[row_in(d), row_in(SSD_W), row_in(ATTN_W), row_in(2 * d),
                  _full(wso.shape), _full(wao.shape), _full(wout.shape),
                  _full(fnw.shape), _full(wr.shape), _full(br.shape), _full(tril_strict.shape)],
        out_specs=[row(d), row(SC_ROW_WORDS), row(SC_ROW_WORDS), row(LANES),
                   pl.BlockSpec((SUBLANES, tm), lambda i: (0, i)), _full((SUBLANES, LANES))],
        out_shape=[jax.ShapeDtypeStruct((t, d), F32),
                   jax.ShapeDtypeStruct((t, SC_ROW_WORDS), jnp.uint32),
                   jax.ShapeDtypeStruct((t, SC_ROW_WORDS), jnp.uint32),
                   jax.ShapeDtypeStruct((t, LANES), F32),
                   jax.ShapeDtypeStruct((SUBLANES, t), F32),
                   jax.ShapeDtypeStruct((SUBLANES, LANES), F32)],
        scratch_shapes=[pltpu.VMEM((1, LANES), F32)],
        compiler_params=_params(1),
        name="out_proj",
    )(x2d, ssd, attn, g, wso, wao, wout, fnw, wr, br, tril_strict)


def _sc_mesh():
    return plsc.VectorSubcoreMesh(core_axis_name="core", subcore_axis_name="subcore")


def _sc_scatter_rows(rows, idx_a, idx_b, n_out):
    n, w = rows.shape

    @pl.kernel(out_type=jax.ShapeDtypeStruct((n_out, w), rows.dtype), mesh=_sc_mesh(), name="moe_scatter")
    def scatter(x_hbm, ia_hbm, ib_hbm, o_hbm):
        def body(x_vmem, ia_vmem, ib_vmem):
            pltpu.sync_copy(x_vmem, o_hbm.at[ia_vmem.at[0]])
            pltpu.sync_copy(x_vmem, o_hbm.at[ib_vmem.at[0]])

        pltpu.emit_pipeline(
            body, grid=(n // SC_WINDOW,),
            in_specs=[pl.BlockSpec((SC_WINDOW, w), lambda i: (i, 0)),
                      pl.BlockSpec((1, SC_WINDOW), lambda i: (0, i)),
                      pl.BlockSpec((1, SC_WINDOW), lambda i: (0, i))],
            out_specs=[],
            core_axis_name=("core", "subcore"),
            dimension_semantics=(pltpu.PARALLEL,),
        )(x_hbm, ia_hbm, ib_hbm)

    return scatter(rows, idx_a.reshape(1, n), idx_b.reshape(1, n))


def _sc_gather_rows(table, idx):
    m = idx.shape[0]
    w = table.shape[1]

    @pl.kernel(out_type=jax.ShapeDtypeStruct((m, w), table.dtype), mesh=_sc_mesh(), name="moe_gather")
    def gather(x_hbm, i_hbm, o_hbm):
        def body(i_vmem, o_vmem):
            pltpu.sync_copy(x_hbm.at[i_vmem.at[0]], o_vmem)

        pltpu.emit_pipeline(
            body, grid=(m // SC_WINDOW,),
            in_specs=[pl.BlockSpec((1, SC_WINDOW), lambda i: (0, i))],
            out_specs=[pl.BlockSpec((SC_WINDOW, w), lambda i: (i, 0))],
            core_axis_name=("core", "subcore"),
            dimension_semantics=(pltpu.PARALLEL,),
        )(i_hbm, o_hbm)

    return gather(table, idx.reshape(1, m))


def _gmm_kernel(te_ref, nt_ref, xa_ref, xb_ref, w1_ref, w3_ref, w2_ref, ya_ref, yb_ref, w1s, w3s, w2s):
    i = pl.program_id(0)

    @pl.when(i < nt_ref[0])
    def _():
        @pl.when(jnp.logical_or(i == 0, te_ref[i] != te_ref[jnp.maximum(i - 1, 0)]))
        def _():
            w1s[...] = w1_ref[0].astype(BF16)
            w3s[...] = w3_ref[0].astype(BF16)
            w2s[...] = w2_ref[0].astype(BF16)

        x = _load_packed((xa_ref, xb_ref)).astype(BF16)
        a = _dot(x, w1s[...])
        act = ((a * _sigmoid(a)) * _dot(x, w3s[...])).astype(BF16)
        _store_packed(_dot(act, w2s[...]), (ya_ref, yb_ref))


def _gmm(tile_expert, n_tiles, xs_a, xs_b, w1, w3, w2):
    p, words = xs_a.shape
    tm = TM_MOE
    d, de = w1.shape[1], w1.shape[2]
    row = pl.BlockSpec((tm, words), lambda i, te, nt: (jnp.minimum(i, nt[0] - 1), 0))
    packed = jax.ShapeDtypeStruct((p, words), jnp.uint32)
    return pl.pallas_call(
        _gmm_kernel,
        grid_spec=pltpu.PrefetchScalarGridSpec(
            num_scalar_prefetch=2,
            grid=(p // tm,),
            in_specs=[row, row,
                      pl.BlockSpec((1, d, de), lambda i, te, nt: (te[i], 0, 0)),
                      pl.BlockSpec((1, d, de), lambda i, te, nt: (te[i], 0, 0)),
                      pl.BlockSpec((1, de, d), lambda i, te, nt: (te[i], 0, 0))],
            out_specs=[row, row],
            scratch_shapes=[pltpu.VMEM((d, de), BF16), pltpu.VMEM((d, de), BF16), pltpu.VMEM((de, d), BF16)]),
        out_shape=[packed, packed],
        compiler_params=_params(1),
        name="moe_gmm",
    )(tile_expert, n_tiles, xs_a, xs_b, w1, w3, w2)


def _combine_kernel(x2_ref, y1a_ref, y1b_ref, y2a_ref, y2b_ref, route_ref, o_ref):
    route = route_ref[...]
    c1 = route[:, R_C1:R_C1 + 1]
    c2 = route[:, R_C2:R_C2 + 1]
    o_ref[...] = x2_ref[...] + (c1 * _load_packed((y1a_ref, y1b_ref)) + c2 * _load_packed((y2a_ref, y2b_ref)))


def _combine(x2, y12_a, y12_b, route, slab, n_slabs, out_so_far):
    t, d = x2.shape
    tm = TM_MOE
    nb = t // tm
    words = y12_a.shape[1]
    first = pl.BlockSpec((tm, words), lambda i: (i, 0))
    second = pl.BlockSpec((tm, words), lambda i: (i + nb, 0))
    in_specs = [pl.BlockSpec((tm, d), lambda i: (i, 0)), first, first, second, second,
                pl.BlockSpec((tm, LANES), lambda i: (i, 0))]
    args = [x2, y12_a, y12_b, y12_a, y12_b, route]
    kern = _combine_kernel
    aliases = {}
    if out_so_far is not None:
        in_specs.append(pl.BlockSpec(memory_space=pl.ANY))
        args.append(out_so_far)
        aliases = {len(args) - 1: 0}
        kern = lambda *refs: _combine_kernel(*refs[:6], refs[7])
    return pl.pallas_call(
        kern,
        grid=(nb,),
        in_specs=in_specs,
        out_specs=pl.BlockSpec((tm, d), lambda i: (i + slab * nb, 0)),
        out_shape=jax.ShapeDtypeStruct((n_slabs * t, d), F32),
        input_output_aliases=aliases,
        compiler_params=_params(1),
        name="moe_combine",
    )(*args)


def _routed_moe(x2, h2_a, h2_b, route, route_t, counts, w1, w3, w2, slab, n_slabs, out_so_far):
    t = x2.shape[0]
    tm = TM_MOE
    p_max = 2 * t + N_EXPERTS * tm
    experts = jnp.arange(N_EXPERTS, dtype=jnp.int32)
    e12 = route_t[R_E1:R_E2 + 1].astype(jnp.int32)
    r12 = route_t[R_RANK1:R_RANK2 + 1].astype(jnp.int32)
    cnt = counts[0, :N_EXPERTS].astype(jnp.int32)
    padded = (cnt + tm - 1) // tm * tm
    ends = jnp.cumsum(padded)
    starts = ends - padded
    pos = jnp.sum(jnp.where(e12[..., None] == experts, starts, 0), axis=-1) + r12
    tile_ends = ends // tm
    tiles = jnp.arange(p_max // tm, dtype=jnp.int32)
    tile_expert = jnp.minimum(jnp.sum((tiles[:, None] >= tile_ends[None, :]).astype(jnp.int32), axis=1),
                              N_EXPERTS - 1)
    n_tiles = tile_ends[-1:].astype(jnp.int32)

    pos1, pos2 = pos[0], pos[1]
    xs_a = _sc_scatter_rows(h2_a, pos1, pos2, p_max)
    xs_b = _sc_scatter_rows(h2_b, pos1, pos2, p_max)
    y_a, y_b = _gmm(tile_expert, n_tiles, xs_a, xs_b, w1, w3, w2)
    pos12 = pos.reshape(-1)
    return _combine(x2, _sc_gather_rows(y_a, pos12), _sc_gather_rows(y_b, pos12), route,
                    slab, n_slabs, out_so_far)


def _rope_tables(s):
    rows = s // GRID_W
    row = jnp.repeat(jnp.arange(rows, dtype=jnp.int32), GRID_W)
    col = jnp.tile(jnp.arange(GRID_W, dtype=jnp.int32), rows)
    half = HEAD_DIM // 2
    inv_freq = ROPE_THETA ** (-jnp.arange(0, half, 2, dtype=F32) / half)
    ang_r = row.astype(F32)[:, None] * inv_freq[None, :]
    ang_c = col.astype(F32)[:, None] * inv_freq[None, :]
    cos = jnp.concatenate([jnp.cos(ang_r), jnp.cos(ang_r), jnp.cos(ang_c), jnp.cos(ang_c)], axis=-1)
    sin_signed = jnp.concatenate([-jnp.sin(ang_r), jnp.sin(ang_r), -jnp.sin(ang_c), jnp.sin(ang_c)], axis=-1)
    return cos, sin_signed


def _pad_lanes(v):
    return jnp.pad(v, (0, LANES - v.shape[0]))[None]


def kernel(x, norm_mix_w, w_in, b_gate, q_norm_w, k_norm_w, w_attn_o, conv_w, conv_b, dt_bias, a_log, d_skip,
           ssd_norm_w, w_ssd_o, w_out, norm_ffn_w, w_router_group, b_router_group, w_router_expert,
           b_router_expert, w1, w3, w2):
    b, s, d = x.shape
    t = b * s
    depth = norm_mix_w.shape[0]
    cos, sin_signed = _rope_tables(s)
    x2d = x.reshape(t, d)
    sizes = (ATTN_W, KV_W, KV_W, SSD_W, CONV_CH, 2 * SSD_HEADS, 2 * d)
    offs = [0]
    for n in sizes:
        offs.append(offs[-1] + n)

    for l in range(depth):
        wi = w_in[l].astype(BF16)
        wq, wk, wv, wz, wxbc, wdt, wg = (wi[:, offs[j]:offs[j + 1]] for j in range(7))
        wdt = jnp.pad(wdt, ((0, 0), (0, LANES - 2 * SSD_HEADS)))
        wrest = jnp.concatenate([wq, wk, wv, wz, wg, wdt], axis=1)
        cw = jnp.pad(conv_w[l], ((0, SUBLANES - D_CONV), (0, 0)))
        q, k, v, z, xc, dt_raw, g = _in_proj(x2d, norm_mix_w[l][None], wrest, wxbc, b_gate[l][None],
                                             cw, conv_b[l][None], s)

        attn = _attention(q.reshape(b, s, ATTN_W), k, v, cos, sin_signed, q_norm_w[l][None], k_norm_w[l][None],
                          b, s)

        ssd = _ssd(xc.reshape(b, s, CONV_CH), dt_raw.reshape(b, s, LANES), z.reshape(b, s, SSD_W),
                   _pad_lanes(dt_bias[l].reshape(-1)), _pad_lanes(a_log[l].reshape(-1)),
                   jnp.repeat(a_log[l], SSD_HEAD_DIM, axis=1)[:, None, :],
                   jnp.repeat(d_skip[l], SSD_HEAD_DIM)[None], ssd_norm_w[l][None], b, s)

        wr = jnp.concatenate([w_router_expert[l], w_router_group[l]], axis=1)
        wr = jnp.pad(wr, ((0, 0), (0, LANES - wr.shape[1])))
        wr_hi = wr.astype(BF16)
        wr_lo = (wr - wr_hi.astype(F32)).astype(BF16)
        br = _pad_lanes(jnp.concatenate([b_router_expert[l], b_router_group[l]]))
        proj_w = (w_ssd_o[l].astype(BF16), w_attn_o[l].astype(BF16), w_out[l].astype(BF16),
                  norm_ffn_w[l][None], jnp.concatenate([wr_hi, wr_lo], axis=1), br)
        out = None
        for slab in range(MOE_SLABS):
            x2, h2_a, h2_b, route, route_t, counts = _out_proj(
                x2d, ssd.reshape(t, SSD_W), attn.reshape(t, ATTN_W), g, *proj_w, slab, MOE_SLABS)
            out = _routed_moe(x2, h2_a, h2_b, route, route_t, counts, w1[l], w3[l], w2[l],
                              slab, MOE_SLABS, out)
        x2d = out
    return x2d.reshape(b, s, d)
```

```python
import functools
import math

import jax
import jax.numpy as jnp
from jax import lax
from jax.experimental import pallas as pl
from jax.experimental.pallas import tpu as pltpu
from jax.experimental.pallas import tpu_sc as plsc

GRID_W = 64
HEAD_DIM = 64
N_Q_HEADS = 8
N_KV_HEADS = 2
GQA_GROUP = N_Q_HEADS // N_KV_HEADS
ATTN_W = N_Q_HEADS * HEAD_DIM
KV_W = N_KV_HEADS * HEAD_DIM
ROPE_THETA = 10000.0
SSD_HEAD_DIM = 64
SSD_HEADS = 8
SSD_W = SSD_HEADS * SSD_HEAD_DIM
SSD_GROUPS = 2
HEADS_PER_GROUP = SSD_HEADS // SSD_GROUPS
GROUP_W = HEADS_PER_GROUP * SSD_HEAD_DIM
D_STATE = 64
D_CONV = 7
CONV_PAD = (D_CONV - 1) // 2
CONV_CH = SSD_W + 2 * SSD_GROUPS * D_STATE
CHUNK = 128
N_EXPERT_GROUPS = 4
EXPERTS_PER_GROUP = 4
N_EXPERTS = N_EXPERT_GROUPS * EXPERTS_PER_GROUP
EPS = 1e-6

LANES = 128
SUBLANES = 8
VMEM_LIMIT = 56 * 1024 * 1024

SC_WINDOW = 128
SC_ROW_WORDS = 256

TM_PROJ = 512
TM_OUT = 1024
PROJ_SPLIT = 8
TQ = 512
KEY_BLOCK = 2048
BOUND_SHIFT_LIMIT = 50.0
BOUND_MARGIN = 1.02
SSD_BATCH = 2
TM_MOE = 512
MOE_SLABS = 1

F32 = jnp.float32
BF16 = jnp.bfloat16
HIGHEST = lax.Precision.HIGHEST


def _dot(a, b):
    return jnp.dot(a, b, preferred_element_type=F32)


def _dot_nt(a, b):
    return lax.dot_general(a, b, (((1,), (1,)), ((), ())), preferred_element_type=F32)


def _dot_tn(a, b):
    return lax.dot_general(a, b, (((0,), (0,)), ((), ())), preferred_element_type=F32)


def _sigmoid(x):
    return 1.0 / (1.0 + jnp.exp(-x))


def _softplus(x):
    return jnp.maximum(x, 0.0) + jnp.log1p(jnp.exp(-jnp.abs(x)))


def _params(n_axes, **flags):
    return pltpu.CompilerParams(dimension_semantics=("arbitrary",) * n_axes,
                                vmem_limit_bytes=VMEM_LIMIT, flags=flags or None)


def _full(shape):
    return pl.BlockSpec(shape, lambda *_: (0,) * len(shape))


def _in_proj_kernel(tiles_per_seq, x_ref, xprev_ref, xnext_ref, nw_ref, wrest_ref, wxbc_ref, bg_ref, cw_ref, cb_ref,
                    q_ref, k_ref, v_ref, z_ref, xc_ref, dt_ref, g_ref):
    tm = x_ref.shape[0]

    def normed(x):
        ms = jnp.mean(x * x, axis=-1, keepdims=True)
        return (x * lax.rsqrt(ms + EPS) * nw_ref[...]).astype(BF16)

    h = normed(x_ref[...])

    pos = pl.program_id(0) % tiles_per_seq
    h_halo = normed(jnp.concatenate([xprev_ref[...], xnext_ref[...]], axis=0))
    xbc = _dot(jnp.concatenate([h, h_halo], axis=0), wxbc_ref[...])
    prev = jnp.where(pos > 0, xbc[tm:tm + SUBLANES], 0.0)
    nxt = jnp.where(pos < tiles_per_seq - 1, xbc[tm + SUBLANES:], 0.0)
    xin = jnp.concatenate([prev, xbc[:tm], nxt], axis=0)
    acc = jnp.zeros((tm, CONV_CH), F32) + cb_ref[...]
    for t in range(D_CONV):
        off = SUBLANES - CONV_PAD + t
        acc = acc + xin[off:off + tm, :] * cw_ref[t:t + 1, :]
    xc_ref[...] = acc * _sigmoid(acc)

    r = _dot(h, wrest_ref[...])
    edges = [0]
    for width in (ATTN_W, KV_W, KV_W, SSD_W, g_ref.shape[1], LANES):
        edges.append(edges[-1] + width)
    q, k, v, z, g, dt = (r[:, a:b] for a, b in zip(edges[:-1], edges[1:]))
    q_ref[...] = q
    v = v.astype(BF16)
    for j in range(N_KV_HEADS):
        k_ref[j] = k[:, j * HEAD_DIM:(j + 1) * HEAD_DIM]
        v_ref[j] = v[:, j * HEAD_DIM:(j + 1) * HEAD_DIM]
    z_ref[...] = z
    dt_ref[...] = dt
    g_ref[...] = _sigmoid(g + bg_ref[...]).astype(BF16)


def _in_proj(x2d, nw, wrest, wxbc, bg, conv_w, conv_b, seq_len):
    t, d = x2d.shape
    tm = TM_PROJ
    hb = tm // SUBLANES
    row = lambda n: pl.BlockSpec((tm, n), lambda i: (i, 0))
    return pl.pallas_call(
        functools.partial(_in_proj_kernel, seq_len // tm),
        grid=(t // tm,),
        in_specs=[row(d),
                  pl.BlockSpec((SUBLANES, d), lambda i: (jnp.maximum(i * hb - 1, 0), 0)),
                  pl.BlockSpec((SUBLANES, d), lambda i: (jnp.minimum((i + 1) * hb, t // SUBLANES - 1), 0)),
                  _full(nw.shape), _full(wrest.shape), _full(wxbc.shape), _full(bg.shape),
                  _full(conv_w.shape), _full(conv_b.shape)],
        out_specs=[row(ATTN_W),
                   pl.BlockSpec((N_KV_HEADS, tm, HEAD_DIM), lambda i: (0, i, 0)),
                   pl.BlockSpec((N_KV_HEADS, tm, HEAD_DIM), lambda i: (0, i, 0)),
                   row(SSD_W), row(CONV_CH), row(LANES), row(bg.shape[1])],
        out_shape=[jax.ShapeDtypeStruct((t, ATTN_W), F32),
                   jax.ShapeDtypeStruct((N_KV_HEADS, t, HEAD_DIM), F32),
                   jax.ShapeDtypeStruct((N_KV_HEADS, t, HEAD_DIM), BF16),
                   jax.ShapeDtypeStruct((t, SSD_W), F32),
                   jax.ShapeDtypeStruct((t, CONV_CH), F32),
                   jax.ShapeDtypeStruct((t, LANES), F32),
                   jax.ShapeDtypeStruct((t, bg.shape[1]), BF16)],
        compiler_params=_params(1),
        name="in_proj",
    )(x2d, x2d, x2d, nw, wrest, wxbc, bg, conv_w, conv_b)


def _rope_partner(u):
    q = HEAD_DIM // 4
    return jnp.concatenate([u[:, q:2 * q], u[:, 0:q], u[:, 3 * q:4 * q], u[:, 2 * q:3 * q]], axis=-1)


def _norm_rope(u, w, cos, sin_signed):
    ms = jnp.mean(u * u, axis=-1, keepdims=True)
    un = u * lax.rsqrt(ms + EPS) * w
    return un * cos + _rope_partner(un) * sin_signed


def _transpose_narrow(x):
    if x.shape[1] == HEAD_DIM:
        return jnp.transpose(jnp.concatenate([x, jnp.zeros_like(x)], axis=1))[:HEAD_DIM, :]
    assert x.shape[0] == HEAD_DIM
    return jnp.transpose(jnp.concatenate([x, jnp.zeros_like(x)], axis=0))[:, :HEAD_DIM]


def _attn_kernel(q_ref, k_ref, v_ref, cosq_ref, sinq_ref, cosk_ref, sink_ref, qw_ref, kw_ref,
                 o_ref, ks_ref, vt_ref, acc_ref, l_ref):
    @pl.when(pl.program_id(2) == 0)
    def _():
        ks_ref[...] = _norm_rope(k_ref[0], kw_ref[...], cosk_ref[...], sink_ref[...]).astype(BF16)
        vt_ref[...] = _transpose_narrow(v_ref[0].astype(F32)).astype(BF16)

    tq = q_ref.shape[1]
    scale = math.log2(math.e) / math.sqrt(HEAD_DIM)
    q_t = jnp.transpose(q_ref[0])
    cos_t = cosq_ref[...]
    sin_t = sinq_ref[...]
    quarter = HEAD_DIM // 4
    heads = []
    for g in range(GQA_GROUP):
        u = q_t[g * HEAD_DIM:(g + 1) * HEAD_DIM, :]
        un = u * lax.rsqrt(jnp.mean(u * u, axis=0, keepdims=True) + EPS) * qw_ref[...]
        partner = jnp.concatenate([un[quarter:2 * quarter], un[:quarter],
                                   un[3 * quarter:], un[2 * quarter:3 * quarter]], axis=0)
        heads.append((un * cos_t + partner * sin_t) * scale)
    qt = jnp.concatenate(heads, axis=-1).astype(BF16)
    n_keys = ks_ref.shape[0]
    kb = min(KEY_BLOCK, n_keys)
    blocks = range(0, n_keys, kb)

    bound = (BOUND_MARGIN * scale * HEAD_DIM
             * jnp.max(jnp.abs(qw_ref[...])) * jnp.max(jnp.abs(kw_ref[...])))
    use_bound = bound < BOUND_SHIFT_LIMIT

    @pl.when(use_bound)
    def _():
        l = acc = None
        for k0 in blocks:
            p = jnp.exp2(_dot(ks_ref[k0:k0 + kb, :], qt) - bound)
            lj = jnp.sum(p, axis=0, keepdims=True)
            oj = _dot(vt_ref[:, k0:k0 + kb], p.astype(BF16))
            l, acc = (lj, oj) if l is None else (l + lj, acc + oj)
        l_ref[...] = l
        acc_ref[...] = acc

    @pl.when(jnp.logical_not(use_bound))
    def _():
        m = l = acc = None
        for k0 in blocks:
            s = _dot(ks_ref[k0:k0 + kb, :], qt)
            mj = jnp.max(s, axis=0, keepdims=True)
            m_new = mj if m is None else jnp.maximum(m, mj)
            p = jnp.exp2(s - m_new)
            lj = jnp.sum(p, axis=0, keepdims=True)
            oj = _dot(vt_ref[:, k0:k0 + kb], p.astype(BF16))
            if m is None:
                l, acc = lj, oj
            else:
                alpha = jnp.exp2(m - m_new)
                l, acc = l * alpha + lj, acc * alpha + oj
            m = m_new
        l_ref[...] = l
        acc_ref[...] = acc

    o = acc_ref[...] / l_ref[...]
    o_ref[0] = jnp.transpose(
        jnp.concatenate([o[:, g * tq:(g + 1) * tq] for g in range(GQA_GROUP)], axis=0)).astype(BF16)


def _attention(q, k, v, cos, sin_signed, qw, kw, b, s):
    tq = TQ
    gw = GQA_GROUP * HEAD_DIM
    return pl.pallas_call(
        _attn_kernel,
        grid=(b, N_KV_HEADS, s // tq),
        in_specs=[pl.BlockSpec((1, tq, gw), lambda bi, kv, qi: (bi, qi, kv)),
                  pl.BlockSpec((1, s, HEAD_DIM), lambda bi, kv, qi: (kv, bi, 0)),
                  pl.BlockSpec((1, s, HEAD_DIM), lambda bi, kv, qi: (kv, bi, 0)),
                  pl.BlockSpec((HEAD_DIM, tq), lambda bi, kv, qi: (0, qi)),
                  pl.BlockSpec((HEAD_DIM, tq), lambda bi, kv, qi: (0, qi)),
                  _full((s, HEAD_DIM)), _full((s, HEAD_DIM)),
                  _full((HEAD_DIM, 1)), _full((1, HEAD_DIM))],
        out_specs=pl.BlockSpec((1, tq, gw), lambda bi, kv, qi: (bi, qi, kv)),
        out_shape=jax.ShapeDtypeStruct((b, s, ATTN_W), BF16),
        scratch_shapes=[pltpu.VMEM((s, HEAD_DIM), BF16), pltpu.VMEM((HEAD_DIM, s), BF16),
                        pltpu.VMEM((HEAD_DIM, GQA_GROUP * tq), F32), pltpu.VMEM((1, GQA_GROUP * tq), F32)],
        compiler_params=_params(3),
        name="attention",
    )(q, k, v, cos.T, sin_signed.T, cos, sin_signed, qw.T, kw)


def _split3(a):
    a1 = a.astype(BF16)
    r1 = a - a1.astype(F32)
    a2 = r1.astype(BF16)
    a3 = (r1 - a2.astype(F32)).astype(BF16)
    return a1, a2, a3


def _scan_chunk(direction, xc, dtraw, bias, alog, alog_x, tri_incl, tri_mask, expand, states):
    dt_all = _softplus(dtraw + bias)
    h0 = direction * SSD_HEADS
    dta_t = jnp.transpose(dt_all * (-jnp.exp(alog)))[h0:h0 + SSD_HEADS, :]
    u_r = jnp.dot(dta_t, tri_incl, precision=HIGHEST, preferred_element_type=F32)
    yield
    dt_x = sum(_dot(piece, expand) for piece in _split3(dt_all))
    yield
    dta_x = dt_x * (-jnp.exp(alog_x))
    tri16 = tri_mask.astype(BF16)
    u_x = sum(_dot(tri16, piece) for piece in _split3(dta_x))
    tot_x = jnp.sum(dta_x, axis=0, keepdims=True)
    yield
    xdt = xc[:, :SSD_W] * dt_x
    xdt16 = xdt.astype(BF16)
    xw16 = (xdt * jnp.exp(tot_x - u_x)).astype(BF16)
    e_u = jnp.exp(u_x)
    keep = jnp.exp(tot_x)
    yield

    def group(g):
        gcols = slice(g * GROUP_W, (g + 1) * GROUP_W)
        bg16 = xc[:, SSD_W + g * D_STATE:SSD_W + (g + 1) * D_STATE].astype(BF16)
        cg16 = xc[:, SSD_W + (SSD_GROUPS + g) * D_STATE:SSD_W + (SSD_GROUPS + g + 1) * D_STATE].astype(BF16)
        cb = _dot_nt(cg16, bg16)
        st = states[g]
        yield
        y_diag = []
        for hh in range(HEADS_PER_GROUP):
            h = g * HEADS_PER_GROUP + hh
            hcols = slice(h * SSD_HEAD_DIM, (h + 1) * SSD_HEAD_DIM)
            uc = u_x[:, h * SSD_HEAD_DIM:h * SSD_HEAD_DIM + 1]
            decay = jnp.exp(jnp.where(tri_mask > 0.0, uc - u_r[h:h + 1, :], -jnp.inf))
            y_diag.append(_dot((cb * decay).astype(BF16), xdt16[:, hcols]))
            yield
        y = jnp.concatenate(y_diag, axis=-1) + _dot(cg16, st.astype(BF16)) * e_u[:, gcols]
        return y, st * keep[:, gcols] + _dot_tn(bg16, xw16[:, gcols])

    gens = [group(g) for g in range(SSD_GROUPS)]
    results = [None] * SSD_GROUPS
    live = list(range(SSD_GROUPS))
    while live:
        for g in list(live):
            try:
                next(gens[g])
            except StopIteration as done:
                results[g] = done.value
                live.remove(g)
            yield
    return jnp.concatenate([r[0] for r in results], axis=-1), [r[1] for r in results]


def _lockstep(*stages):
    results = [None] * len(stages)
    live = list(range(len(stages)))
    while live:
        for i in list(live):
            try:
                next(stages[i])
            except StopIteration as done:
                results[i] = done.value
                live.remove(i)
    return results


def _gate_norm(y, z, w):
    y = y * (z * _sigmoid(z))
    ms = jnp.mean(y * y, axis=-1, keepdims=True)
    return (y * lax.rsqrt(ms + EPS) * w).astype(BF16)


def _ssd_kernel(nc, xcf_ref, xcb_ref, dtf_ref, dtb_ref, zf_ref, zb_ref, bias_ref, alog_ref, alogx_ref,
                mlow_ref, mup_ref, exp_ref, dskip_ref, nw_ref, o_ref, yacc_ref, *state_refs):
    c = pl.program_id(1)
    cbk = nc - 1 - c
    first_visit = c < nc // 2
    rows_f = pl.ds(pl.multiple_of(c * CHUNK, CHUNK), CHUNK)
    rows_b = pl.ds(pl.multiple_of(cbk * CHUNK, CHUNK), CHUNK)

    @pl.when(c == 0)
    def _():
        for ref in state_refs:
            ref[...] = jnp.zeros_like(ref)

    mlow = mlow_ref[...]
    mup = mup_ref[...]
    nb = xcf_ref.shape[0]
    states = [ref[...] for ref in state_refs]
    per = 2 * SSD_GROUPS
    scans = []
    for i in range(nb):
        st = states[i * per:(i + 1) * per]
        scans.append(_scan_chunk(0, xcf_ref[i], dtf_ref[i], bias_ref[...], alog_ref[...], alogx_ref[0],
                                 mup, mlow, exp_ref[0], st[:SSD_GROUPS]))
        scans.append(_scan_chunk(1, xcb_ref[i], dtb_ref[i], bias_ref[...], alog_ref[...], alogx_ref[1],
                                 mlow, mup, exp_ref[1], st[SSD_GROUPS:]))
    results = _lockstep(*scans)
    for ref, st in zip(state_refs, [s_ for _, new in results for s_ in new]):
        ref[...] = st
    ys = [(results[2 * i][0] + dskip_ref[...] * xcf_ref[i][:, :SSD_W], results[2 * i + 1][0]) for i in range(nb)]

    @pl.when(first_visit)
    def _():
        for i, (yf, yb) in enumerate(ys):
            yacc_ref[i, rows_f, :] = yf
            yacc_ref[i, rows_b, :] = yb

    @pl.when(jnp.logical_not(first_visit))
    def _():
        for i, (yf, yb) in enumerate(ys):
            o_ref[i, rows_f, :] = _gate_norm(yacc_ref[i, rows_f, :] + yf, zf_ref[i], nw_ref[...])
            o_ref[i, rows_b, :] = _gate_norm(yacc_ref[i, rows_b, :] + yb, zb_ref[i], nw_ref[...])


def _ssd(xc, dt_raw, z, bias, alog, alog_x, dskip, nw, b, s):
    nc = s // CHUNK
    assert nc % 2 == 0
    tri = jnp.tril(jnp.ones((CHUNK, CHUNK), F32))
    lane_head = jnp.arange(LANES)[None, :, None] - SSD_HEADS * jnp.arange(2)[:, None, None]
    expand = (lane_head == (jnp.arange(SSD_W) // SSD_HEAD_DIM)[None, None, :]).astype(BF16)
    nb = SSD_BATCH if b % SSD_BATCH == 0 else 1
    chunk_f = lambda n: pl.BlockSpec((nb, CHUNK, n), lambda bi, c: (bi, c, 0))
    chunk_b = lambda n: pl.BlockSpec((nb, CHUNK, n), lambda bi, c: (bi, nc - 1 - c, 0))
    return pl.pallas_call(
        functools.partial(_ssd_kernel, nc),
        grid=(b // nb, nc),
        in_specs=[chunk_f(CONV_CH), chunk_b(CONV_CH), chunk_f(LANES), chunk_b(LANES),
                  chunk_f(SSD_W), chunk_b(SSD_W), _full(bias.shape), _full(alog.shape),
                  _full(alog_x.shape), _full(tri.shape), _full(tri.shape), _full(expand.shape),
                  _full(dskip.shape), _full(nw.shape)],
        out_specs=pl.BlockSpec((nb, s, SSD_W), lambda bi, c: (bi, 0, 0)),
        out_shape=jax.ShapeDtypeStruct((b, s, SSD_W), BF16),
        scratch_shapes=[pltpu.VMEM((nb, s, SSD_W), F32)]
                       + [pltpu.VMEM((D_STATE, GROUP_W), F32)] * (nb * 2 * SSD_GROUPS),
        compiler_params=_params(2),
        name="ssd",
    )(xc, xc, dt_raw, dt_raw, z, z, bias, alog, alog_x, tri, tri.T, expand, dskip, nw)


def _pack_bf16_pairs(v):
    n = v.shape[1] // 2
    hi = pltpu.bitcast(v[:, :n].astype(BF16).astype(F32), jnp.uint32)
    lo = pltpu.bitcast(v[:, n:].astype(BF16).astype(F32), jnp.uint32)
    return hi | (lo >> 16)


def _unpack_bf16_pairs(u):
    hi = pltpu.bitcast(u & jnp.uint32(0xFFFF0000), F32)
    lo = pltpu.bitcast(u << 16, F32)
    return hi, lo


def _store_packed(v, refs):
    words = _pack_bf16_pairs(v)
    w = refs[0].shape[-1]
    for j, ref in enumerate(refs):
        ref[...] = words[:, j * w:(j + 1) * w]


def _load_packed(refs):
    parts = [_unpack_bf16_pairs(ref[...]) for ref in refs]
    return jnp.concatenate([p[0] for p in parts] + [p[1] for p in parts], axis=-1)


R_E1, R_E2, R_C1, R_C2, R_RANK1, R_RANK2 = range(6)


def _out_proj_kernel(x_ref, ssd_ref, attn_ref, g_ref, wso_ref, wao_ref, wout_ref,
                     fnw_ref, wr_ref, br_ref, tril_ref, x2_ref, h2a_ref, h2b_ref, route_ref, route_t_ref,
                     cnt_ref, base_ref):
    @pl.when(pl.program_id(0) == 0)
    def _():
        base_ref[...] = jnp.zeros_like(base_ref)

    dm = x_ref.shape[1]
    rb = x_ref.shape[0] // PROJ_SPLIT

    def block(rows):
        g = g_ref[rows, :]
        merged = (g[:, :dm] * _dot(attn_ref[rows, :], wao_ref[...])
                  + g[:, dm:] * _dot(ssd_ref[rows, :], wso_ref[...]))
        yield
        x2 = x_ref[rows, :] + _dot(merged.astype(BF16), wout_ref[...])
        x2_ref[rows, :] = x2
        yield
        ms2 = jnp.mean(x2 * x2, axis=-1, keepdims=True)
        h2 = x2 * lax.rsqrt(ms2 + EPS) * fnw_ref[...]
        h2_hi = h2.astype(BF16)
        _store_packed(h2, (h2a_ref.at[rows], h2b_ref.at[rows]))
        yield
        h2_lo = (h2 - h2_hi.astype(F32)).astype(BF16)
        pp = _dot(h2_hi, wr_ref[...]) + _dot(h2_lo, wr_ref[...])
        lg = pp[:, :LANES] + pp[:, LANES:] + br_ref[...]
        yield
        lane = lax.broadcasted_iota(jnp.int32, lg.shape, 1)
        neg = -jnp.inf
        big = jnp.int32(LANES)
        is_g = (lane >= N_EXPERTS) & (lane < N_EXPERTS + N_EXPERT_GROUPS)
        gl = jnp.where(is_g, lg, neg)
        ge = jnp.exp(gl - jnp.max(gl, axis=-1, keepdims=True))
        pg = ge / jnp.sum(ge, axis=-1, keepdims=True)
        g_val = jnp.max(pg, axis=-1, keepdims=True)
        g_idx = jnp.min(jnp.where(is_g & (pg == g_val), lane, big), axis=-1, keepdims=True) - N_EXPERTS
        yield
        lo = g_idx * EXPERTS_PER_GROUP
        sel = (lane >= lo) & (lane < lo + EXPERTS_PER_GROUP)
        fl = jnp.where(sel, lg, neg)
        fe = jnp.exp(fl - jnp.max(fl, axis=-1, keepdims=True))
        pf = fe / jnp.sum(fe, axis=-1, keepdims=True)
        v1 = jnp.max(pf, axis=-1, keepdims=True)
        i1 = jnp.min(jnp.where(sel & (pf == v1), lane, big), axis=-1, keepdims=True)
        yield
        pf2 = jnp.where(sel & (lane != i1), pf, -1.0)
        v2 = jnp.max(pf2, axis=-1, keepdims=True)
        i2 = jnp.min(jnp.where(pf2 == v2, lane, big), axis=-1, keepdims=True)
        den = v1 + v2
        oh = jnp.where(lane == i1, 1.0, 0.0) + jnp.where(lane == i2, 1.0, 0.0)
        return lane, i1, i2, g_val * (v1 / den), g_val * (v2 / den), oh

    blocks = [pl.ds(r * rb, rb) for r in range(PROJ_SPLIT)]
    base = base_ref[...]
    for rows, (lane, i1, i2, c1, c2, oh) in zip(blocks, _lockstep(*[block(rows) for rows in blocks])):
        cnt = _dot(tril_ref[:rb, :rb], oh.astype(BF16)) + base
        r1 = jnp.sum(jnp.where(lane == i1, cnt, 0.0), axis=-1, keepdims=True)
        r2 = jnp.sum(jnp.where(lane == i2, cnt, 0.0), axis=-1, keepdims=True)
        base = base + jnp.sum(oh, axis=0, keepdims=True)
        rec = jnp.zeros(oh.shape, F32)
        for slot, val in ((R_E1, i1.astype(F32)), (R_E2, i2.astype(F32)), (R_C1, c1), (R_C2, c2),
                          (R_RANK1, r1), (R_RANK2, r2)):
            rec = jnp.where(lane == slot, val, rec)
        route_ref[rows, :] = rec
        route_t_ref[:, rows] = jnp.transpose(rec)[:SUBLANES, :]
    base_ref[...] = base
    cnt_ref[...] = jnp.broadcast_to(base, cnt_ref.shape)


def _out_proj(x2d, ssd, attn, g, wso, wao, wout, fnw, wr, br, slab, n_slabs):
    d = x2d.shape[1]
    t = x2d.shape[0] // n_slabs
    tm = min(TM_OUT, t)
    first = slab * (t // tm)
    row_in = lambda n: pl.BlockSpec((tm, n), lambda i: (i + first, 0))
    row = lambda n: pl.BlockSpec((tm, n), lambda i: (i, 0))
    tril_strict = jnp.tril(jnp.ones((tm // PROJ_SPLIT, tm // PROJ_SPLIT), BF16), k=-1)
    return pl.pallas_call(
        _out_proj_kernel,
        grid=(t // tm,),
        in_specs=[row_in(d), row_in(SSD_W), row_in(ATTN_W), row_in(2 * d),
                  _full(wso.shape), _full(wao.shape), _full(wout.shape),
                  _full(fnw.shape), _full(wr.shape), _full(br.shape), _full(tril_strict.shape)],
        out_specs=[row(d), row(SC_ROW_WORDS), row(SC_ROW_WORDS), row(LANES),
                   pl.BlockSpec((SUBLANES, tm), lambda i: (0, i)), _full((SUBLANES, LANES))],
        out_shape=[jax.ShapeDtypeStruct((t, d), F32),
                   jax.ShapeDtypeStruct((t, SC_ROW_WORDS), jnp.uint32),
                   jax.ShapeDtypeStruct((t, SC_ROW_WORDS), jnp.uint32),
                   jax.ShapeDtypeStruct((t, LANES), F32),
                   jax.ShapeDtypeStruct((SUBLANES, t), F32),
                   jax.ShapeDtypeStruct((SUBLANES, LANES), F32)],
        scratch_shapes=[pltpu.VMEM((1, LANES), F32)],
        compiler_params=_params(1),
        name="out_proj",
    )(x2d, ssd, attn, g, wso, wao, wout, fnw, wr, br, tril_strict)


def _sc_mesh():
    return plsc.VectorSubcoreMesh(core_axis_name="core", subcore_axis_name="subcore")


def _sc_scatter_rows(rows, idx_a, idx_b, n_out):
    n, w = rows.shape

    @pl.kernel(out_type=jax.ShapeDtypeStruct((n_out, w), rows.dtype), mesh=_sc_mesh(), name="moe_scatter")
    def scatter(x_hbm, ia_hbm, ib_hbm, o_hbm):
        def body(x_vmem, ia_vmem, ib_vmem):
            pltpu.sync_copy(x_vmem, o_hbm.at[ia_vmem.at[0]])
            pltpu.sync_copy(x_vmem, o_hbm.at[ib_vmem.at[0]])

        pltpu.emit_pipeline(
            body, grid=(n // SC_WINDOW,),
            in_specs=[pl.BlockSpec((SC_WINDOW, w), lambda i: (i, 0)),
                      pl.BlockSpec((1, SC_WINDOW), lambda i: (0, i)),
                      pl.BlockSpec((1, SC_WINDOW), lambda i: (0, i))],
            out_specs=[],
            core_axis_name=("core", "subcore"),
            dimension_semantics=(pltpu.PARALLEL,),
        )(x_hbm, ia_hbm, ib_hbm)

    return scatter(rows, idx_a.reshape(1, n), idx_b.reshape(1, n))


def _sc_gather_rows(table, idx):
    m = idx.shape[0]
    w = table.shape[1]

    @pl.kernel(out_type=jax.ShapeDtypeStruct((m, w), table.dtype), mesh=_sc_mesh(), name="moe_gather")
    def gather(x_hbm, i_hbm, o_hbm):
        def body(i_vmem, o_vmem):
            pltpu.sync_copy(x_hbm.at[i_vmem.at[0]], o_vmem)

        pltpu.emit_pipeline(
            body, grid=(m // SC_WINDOW,),
            in_specs=[pl.BlockSpec((1, SC_WINDOW), lambda i: (0, i))],
            out_specs=[pl.BlockSpec((SC_WINDOW, w), lambda i: (i, 0))],
            core_axis_name=("core", "subcore"),
            dimension_semantics=(pltpu.PARALLEL,),
        )(i_hbm, o_hbm)

    return gather(table, idx.reshape(1, m))


def _gmm_kernel(te_ref, nt_ref, xa_ref, xb_ref, w1_ref, w3_ref, w2_ref, ya_ref, yb_ref, w1s, w3s, w2s):
    i = pl.program_id(0)

    @pl.when(i < nt_ref[0])
    def _():
        @pl.when(jnp.logical_or(i == 0, te_ref[i] != te_ref[jnp.maximum(i - 1, 0)]))
        def _():
            w1s[...] = w1_ref[0].astype(BF16)
            w3s[...] = w3_ref[0].astype(BF16)
            w2s[...] = w2_ref[0].astype(BF16)

        x = _load_packed((xa_ref, xb_ref)).astype(BF16)
        a = _dot(x, w1s[...])
        act = ((a * _sigmoid(a)) * _dot(x, w3s[...])).astype(BF16)
        _store_packed(_dot(act, w2s[...]), (ya_ref, yb_ref))


def _gmm(tile_expert, n_tiles, xs_a, xs_b, w1, w3, w2):
    p, words = xs_a.shape
    tm = TM_MOE
    d, de = w1.shape[1], w1.shape[2]
    row = pl.BlockSpec((tm, words), lambda i, te, nt: (jnp.minimum(i, nt[0] - 1), 0))
    packed = jax.ShapeDtypeStruct((p, words), jnp.uint32)
    return pl.pallas_call(
        _gmm_kernel,
        grid_spec=pltpu.PrefetchScalarGridSpec(
            num_scalar_prefetch=2,
            grid=(p // tm,),
            in_specs=[row, row,
                      pl.BlockSpec((1, d, de), lambda i, te, nt: (te[i], 0, 0)),
                      pl.BlockSpec((1, d, de), lambda i, te, nt: (te[i], 0, 0)),
                      pl.BlockSpec((1, de, d), lambda i, te, nt: (te[i], 0, 0))],
            out_specs=[row, row],
            scratch_shapes=[pltpu.VMEM((d, de), BF16), pltpu.VMEM((d, de), BF16), pltpu.VMEM((de, d), BF16)]),
        out_shape=[packed, packed],
        compiler_params=_params(1),
        name="moe_gmm",
    )(tile_expert, n_tiles, xs_a, xs_b, w1, w3, w2)


def _combine_kernel(x2_ref, y1a_ref, y1b_ref, y2a_ref, y2b_ref, route_ref, o_ref):
    route = route_ref[...]
    c1 = route[:, R_C1:R_C1 + 1]
    c2 = route[:, R_C2:R_C2 + 1]
    o_ref[...] = x2_ref[...] + (c1 * _load_packed((y1a_ref, y1b_ref)) + c2 * _load_packed((y2a_ref, y2b_ref)))


def _combine(x2, y12_a, y12_b, route, slab, n_slabs, out_so_far):
    t, d = x2.shape
    tm = TM_MOE
    nb = t // tm
    words = y12_a.shape[1]
    first = pl.BlockSpec((tm, words), lambda i: (i, 0))
    second = pl.BlockSpec((tm, words), lambda i: (i + nb, 0))
    in_specs = [pl.BlockSpec((tm, d), lambda i: (i, 0)), first, first, second, second,
                pl.BlockSpec((tm, LANES), lambda i: (i, 0))]
    args = [x2, y12_a, y12_b, y12_a, y12_b, route]
    kern = _combine_kernel
    aliases = {}
    if out_so_far is not None:
        in_specs.append(pl.BlockSpec(memory_space=pl.ANY))
        args.append(out_so_far)
        aliases = {len(args) - 1: 0}
        kern = lambda *refs: _combine_kernel(*refs[:6], refs[7])
    return pl.pallas_call(
        kern,
        grid=(nb,),
        in_specs=in_specs,
        out_specs=pl.BlockSpec((tm, d), lambda i: (i + slab * nb, 0)),
        out_shape=jax.ShapeDtypeStruct((n_slabs * t, d), F32),
        input_output_aliases=aliases,
        compiler_params=_params(1),
        name="moe_combine",
    )(*args)


def _routed_moe(x2, h2_a, h2_b, route, route_t, counts, w1, w3, w2, slab, n_slabs, out_so_far):
    t = x2.shape[0]
    tm = TM_MOE
    p_max = 2 * t + N_EXPERTS * tm
    experts = jnp.arange(N_EXPERTS, dtype=jnp.int32)
    e12 = route_t[R_E1:R_E2 + 1].astype(jnp.int32)
    r12 = route_t[R_RANK1:R_RANK2 + 1].astype(jnp.int32)
    cnt = counts[0, :N_EXPERTS].astype(jnp.int32)
    padded = (cnt + tm - 1) // tm * tm
    ends = jnp.cumsum(padded)
    starts = ends - padded
    pos = jnp.sum(jnp.where(e12[..., None] == experts, starts, 0), axis=-1) + r12
    tile_ends = ends // tm
    tiles = jnp.arange(p_max // tm, dtype=jnp.int32)
    tile_expert = jnp.minimum(jnp.sum((tiles[:, None] >= tile_ends[None, :]).astype(jnp.int32), axis=1),
                              N_EXPERTS - 1)
    n_tiles = tile_ends[-1:].astype(jnp.int32)

    pos1, pos2 = pos[0], pos[1]
    xs_a = _sc_scatter_rows(h2_a, pos1, pos2, p_max)
    xs_b = _sc_scatter_rows(h2_b, pos1, pos2, p_max)
    y_a, y_b = _gmm(tile_expert, n_tiles, xs_a, xs_b, w1, w3, w2)
    pos12 = pos.reshape(-1)
    return _combine(x2, _sc_gather_rows(y_a, pos12), _sc_gather_rows(y_b, pos12), route,
                    slab, n_slabs, out_so_far)


def _rope_tables(s):
    rows = s // GRID_W
    row = jnp.repeat(jnp.arange(rows, dtype=jnp.int32), GRID_W)
    col = jnp.tile(jnp.arange(GRID_W, dtype=jnp.int32), rows)
    half = HEAD_DIM // 2
    inv_freq = ROPE_THETA ** (-jnp.arange(0, half, 2, dtype=F32) / half)
    ang_r = row.astype(F32)[:, None] * inv_freq[None, :]
    ang_c = col.astype(F32)[:, None] * inv_freq[None, :]
    cos = jnp.concatenate([jnp.cos(ang_r), jnp.cos(ang_r), jnp.cos(ang_c), jnp.cos(ang_c)], axis=-1)
    sin_signed = jnp.concatenate([-jnp.sin(ang_r), jnp.sin(ang_r), -jnp.sin(ang_c), jnp.sin(ang_c)], axis=-1)
    return cos, sin_signed


def _pad_lanes(v):
    return jnp.pad(v, (0, LANES - v.shape[0]))[None]


def kernel(x, norm_mix_w, w_in, b_gate, q_norm_w, k_norm_w, w_attn_o, conv_w, conv_b, dt_bias, a_log, d_skip,
           ssd_norm_w, w_ssd_o, w_out, norm_ffn_w, w_router_group, b_router_group, w_router_expert,
           b_router_expert, w1, w3, w2):
    b, s, d = x.shape
    t = b * s
    depth = norm_mix_w.shape[0]
    cos, sin_signed = _rope_tables(s)
    x2d = x.reshape(t, d)
    sizes = (ATTN_W, KV_W, KV_W, SSD_W, CONV_CH, 2 * SSD_HEADS, 2 * d)
    offs = [0]
    for n in sizes:
        offs.append(offs[-1] + n)

    for l in range(depth):
        wi = w_in[l].astype(BF16)
        wq, wk, wv, wz, wxbc, wdt, wg = (wi[:, offs[j]:offs[j + 1]] for j in range(7))
        wdt = jnp.pad(wdt, ((0, 0), (0, LANES - 2 * SSD_HEADS)))
        wrest = jnp.concatenate([wq, wk, wv, wz, wg, wdt], axis=1)
        cw = jnp.pad(conv_w[l], ((0, SUBLANES - D_CONV), (0, 0)))
        q, k, v, z, xc, dt_raw, g = _in_proj(x2d, norm_mix_w[l][None], wrest, wxbc, b_gate[l][None],
                                             cw, conv_b[l][None], s)

        attn = _attention(q.reshape(b, s, ATTN_W), k, v, cos, sin_signed, q_norm_w[l][None], k_norm_w[l][None],
                          b, s)

        ssd = _ssd(xc.reshape(b, s, CONV_CH), dt_raw.reshape(b, s, LANES), z.reshape(b, s, SSD_W),
                   _pad_lanes(dt_bias[l].reshape(-1)), _pad_lanes(a_log[l].reshape(-1)),
                   jnp.repeat(a_log[l], SSD_HEAD_DIM, axis=1)[:, None, :],
                   jnp.repeat(d_skip[l], SSD_HEAD_DIM)[None], ssd_norm_w[l][None], b, s)

        wr = jnp.concatenate([w_router_expert[l], w_router_group[l]], axis=1)
        wr = jnp.pad(wr, ((0, 0), (0, LANES - wr.shape[1])))
        wr_hi = wr.astype(BF16)
        wr_lo = (wr - wr_hi.astype(F32)).astype(BF16)
        br = _pad_lanes(jnp.concatenate([b_router_expert[l], b_router_group[l]]))
        proj_w = (w_ssd_o[l].astype(BF16), w_attn_o[l].astype(BF16), w_out[l].astype(BF16),
                  norm_ffn_w[l][None], jnp.concatenate([wr_hi, wr_lo], axis=1), br)
        out = None
        for slab in range(MOE_SLABS):
            x2, h2_a, h2_b, route, route_t, counts = _out_proj(
                x2d, ssd.reshape(t, SSD_W), attn.reshape(t, ATTN_W), g, *proj_w, slab, MOE_SLABS)
            out = _routed_moe(x2, h2_a, h2_b, route, route_t, counts, w1[l], w3[l], w2[l],
                              slab, MOE_SLABS, out)
        x2d = out
    return x2d.reshape(b, s, d)
```
